```python
import math
import jax
import jax.numpy as jnp
from jax import lax
import numpy as np

D_MODEL = 2048
BATCH = 8
SEQ = 2048
DEPTH = 1
DEC_BATCH = 128
DEC_SEQ = 8
PAST_LEN = 2048
PAGE_SIZE = 128

HEAD_DIM = 128
N_HEADS = 8
N_KV = 2
HPG = N_HEADS // N_KV
ATTN_DIM = N_HEADS * HEAD_DIM
KV_DIM = N_KV * HEAD_DIM
MIX_DIM = D_MODEL
CONV_DIM = MIX_DIM - ATTN_DIM
CONV_WIDTH = 3
CMP_BLOCK = 32
CMP_STRIDE = 16
CMP_HID = 2 * HEAD_DIM
SEL_BLOCK = 64
TOP_N = 8
WINDOW = 512
Q_BLOCK = 64
N_BUCKETS = 32
MAX_DIST = 128
N_KEYS = 128
N_EXPERTS = N_KEYS * N_KEYS
PEER_HEADS = 8
PEER_TOPK = 16
PEER_DK = 256
PEER_CHUNK = 128
PLE_DIM = 256
N_SLOTS = 4
N_BRANCH = 3
IN_COLS = ATTN_DIM + 6 * KV_DIM + N_BRANCH * N_HEADS + 3 * CONV_DIM
SPLITS = [ATTN_DIM, ATTN_DIM + 6 * KV_DIM, ATTN_DIM + 6 * KV_DIM + N_BRANCH * N_HEADS]
EPS = 1e-6
NEG = -1e30
FORCE = 1e4

kernel_name = "hybrid_nsa_shortconv_peer_step"


def rmsnorm(x, g):
    x32 = x.astype(jnp.float32)
    y = x32 * lax.rsqrt(jnp.mean(x32 * x32, axis=-1, keepdims=True) + EPS)
    return (y * g.astype(jnp.float32)).astype(x.dtype)


def rel_bucket(dist):
    n = jnp.maximum(dist, 0)
    exact = N_BUCKETS // 2
    nf = jnp.maximum(n, 1).astype(jnp.float32)
    large = exact + (jnp.log(nf / exact) / math.log(MAX_DIST / exact) * (N_BUCKETS - exact)).astype(jnp.int32)
    return jnp.where(n < exact, n, jnp.minimum(large, N_BUCKETS - 1))


def mix_inputs(h, g_norm, w_in, qk_gain):
    b, L, _ = h.shape
    z = rmsnorm(h, g_norm) @ w_in
    q, kv, gt, cv = jnp.split(z, SPLITS, axis=-1)
    q = rmsnorm(q.reshape(b, L, N_HEADS, HEAD_DIM), qk_gain[0])
    kv = kv.reshape(b, L, 6, N_KV, HEAD_DIM)
    rows = jnp.stack([kv[:, :, 0], kv[:, :, 1], rmsnorm(kv[:, :, 2], qk_gain[2]), kv[:, :, 3]], axis=2)
    wrows = jnp.stack([rmsnorm(kv[:, :, 4], qk_gain[3]), kv[:, :, 5]], axis=2)
    gates = jax.nn.sigmoid(gt.reshape(b, L, N_BRANCH, N_HEADS))
    b_gate, c_gate, hc = jnp.split(cv, 3, axis=-1)
    return q, rows, wrows, gates, b_gate, c_gate * hc


def compress(raw, pe, w1, w2):
    b, T = raw.shape[:2]
    m = CMP_BLOCK // CMP_STRIDE
    nch = T // CMP_STRIDE
    ncmp = nch - m + 1
    ch = raw[:, :nch * CMP_STRIDE].reshape(b, nch, CMP_STRIDE, N_KV, HEAD_DIM)
    part = jnp.einsum("bcsgd,msdh->bcmgh", ch, w1.reshape(m, CMP_STRIDE, HEAD_DIM, CMP_HID))
    hid = pe.reshape(-1) @ w1
    for i in range(m):
        hid = hid + part[:, i:i + ncmp, i]
    return jax.nn.gelu(hid) @ w2


def to_blocks(k):
    b, T = k.shape[:2]
    ns = max(-(-T // SEL_BLOCK), TOP_N)
    k = jnp.pad(k, ((0, 0), (0, ns * SEL_BLOCK - T), (0, 0), (0, 0)))
    return k.reshape(b, ns, SEL_BLOCK, N_KV, HEAD_DIM).transpose(0, 3, 1, 2, 4)


def nsa_keys(rows, pe, w1, w2, kc_gain):
    kc = rmsnorm(compress(rows[:, :, 0], pe[0], w1[0], w2[0]), kc_gain)
    vc = compress(rows[:, :, 1], pe[1], w1[1], w2[1])
    kc_end = jnp.arange(kc.shape[1]) * CMP_STRIDE + CMP_BLOCK - 1
    return kc, vc, kc_end, to_blocks(rows[:, :, 2]), to_blocks(rows[:, :, 3])


def nsa_attend(q, q_pos, gates, kc, vc, kc_end, ks_blk, vs_blk, kw, vw, kw_pos, rel_bias):
    b, nq = q.shape[:2]
    ns = ks_blk.shape[2]
    scale = HEAD_DIM ** -0.5
    qg = q.reshape(b, nq, N_KV, HPG, HEAD_DIM)
    tab = rel_bias.astype(jnp.float32)

    def dense(k, v, dist, valid):
        logits = jnp.einsum("bqgrd,bkgd->bgrqk", qg, k).astype(jnp.float32) * scale
        bias = tab[rel_bucket(dist)].transpose(2, 0, 1).reshape(N_KV, HPG, nq, -1)
        pr = jax.nn.softmax(jnp.where(valid, logits + bias, NEG), axis=-1)
        pr = jnp.where(valid, pr, 0.0)
        return jnp.einsum("bgrqk,bkgd->bqgrd", pr.astype(v.dtype), v), pr

    dist_c = q_pos[:, None] - kc_end[None, :]
    out_c, pr_c = dense(kc, vc, dist_c, dist_c >= 0)

    c_start = kc_end - CMP_BLOCK + 1
    s_start = jnp.arange(ns) * SEL_BLOCK
    inter = ((c_start[:, None] < s_start[None, :] + SEL_BLOCK) & (kc_end[:, None] >= s_start[None, :])).astype(jnp.float32)
    imp = jnp.einsum("bgrqn,ns->bgqs", pr_c, inter)
    cur = (q_pos // SEL_BLOCK)[:, None]
    sid = jnp.arange(ns)[None, :]
    imp = jnp.where((sid == 0) | (sid == cur) | (sid == cur - 1), FORCE, imp)
    imp = jnp.where(sid > cur, -jnp.inf, imp)
    _, top = lax.top_k(imp, TOP_N)
    take = jax.vmap(jax.vmap(lambda blk, ix: blk[ix]))
    nk = TOP_N * SEL_BLOCK
    ksel = take(ks_blk, top).reshape(b, N_KV, nq, nk, HEAD_DIM)
    vsel = take(vs_blk, top).reshape(b, N_KV, nq, nk, HEAD_DIM)
    kpos = (top[..., None] * SEL_BLOCK + jnp.arange(SEL_BLOCK)).reshape(b, N_KV, nq, nk)
    dist_s = q_pos[None, None, :, None] - kpos
    valid_s = (dist_s >= 0)[:, :, None]
    logits = jnp.einsum("bqgrd,bgqkd->bgrqk", qg, ksel).astype(jnp.float32) * scale
    bias_s = tab.reshape(N_BUCKETS, N_KV, HPG)[rel_bucket(dist_s), jnp.arange(N_KV)[None, :, None, None]]
    logits = logits + bias_s.transpose(0, 1, 4, 2, 3)
    pr_s = jax.nn.softmax(jnp.where(valid_s, logits, NEG), axis=-1)
    out_s = jnp.einsum("bgrqk,bgqkd->bqgrd", pr_s.astype(vsel.dtype), vsel)

    dist_w = q_pos[:, None] - kw_pos[None, :]
    valid_w = (kw_pos[None, :] >= 0) & (dist_w >= 0) & (dist_w < WINDOW)
    out_w, _ = dense(kw, vw, dist_w, valid_w)

    o = jnp.stack([out_c, out_s, out_w], axis=2)
    g = gates.reshape(b, nq, N_BRANCH, N_KV, HPG, 1).astype(o.dtype)
    return (o * g).sum(axis=2).reshape(b, nq, ATTN_DIM)


def nsa_prompt(q, gates, keys, wrows, rel_bias):
    kc, vc, kc_end, ks_blk, vs_blk = keys
    b, L = q.shape[:2]
    wpad = jnp.pad(wrows, ((0, 0), (WINDOW, 0), (0, 0), (0, 0), (0, 0)))

    def block(i):
        s = i * Q_BLOCK
        qb = lax.dynamic_slice_in_dim(q, s, Q_BLOCK, axis=1)
        gb = lax.dynamic_slice_in_dim(gates, s, Q_BLOCK, axis=1)
        wb = lax.dynamic_slice_in_dim(wpad, s, WINDOW + Q_BLOCK, axis=1)
        q_pos = s + jnp.arange(Q_BLOCK)
        kw_pos = s - WINDOW + jnp.arange(WINDOW + Q_BLOCK)
        return nsa_attend(qb, q_pos, gb, kc, vc, kc_end, ks_blk, vs_blk, wb[:, :, 0], wb[:, :, 1], kw_pos, rel_bias)

    out = lax.map(block, jnp.arange(L // Q_BLOCK))
    return out.transpose(1, 0, 2, 3).reshape(b, L, ATTN_DIM)


def short_conv(u, b_gate, prev, w, bias):
    ctx = jnp.concatenate([prev, u], axis=1)
    y = lax.conv_general_dilated(ctx, w[:, None, :], window_strides=(1,), padding="VALID",
                                 dimension_numbers=("NWC", "WIO", "NWC"), feature_group_count=CONV_DIM)
    return b_gate * (y + bias), ctx[:, -(CONV_WIDTH - 1):]


def merge_out(attn, conv, gain, w_out):
    a = rmsnorm(attn, gain[:ATTN_DIM])
    c = rmsnorm(conv, gain[ATTN_DIM:])
    return jnp.concatenate([a, c], axis=-1) @ w_out


def peer_ffn(n, wq, sub_keys, u_tab, v_tab):
    ntok = n.shape[0]
    nc = -(-ntok // PEER_CHUNK)
    xs = jnp.pad(n, ((0, nc * PEER_CHUNK - ntok), (0, 0))).reshape(nc, PEER_CHUNK, D_MODEL)
    ncand = PEER_TOPK * PEER_TOPK

    def chunk(xc):
        qh = (xc @ wq).reshape(PEER_CHUNK, PEER_HEADS, 2, PEER_DK // 2)
        s = jnp.einsum("trhk,hnk->trhn", qh, sub_keys).astype(jnp.float32)
        sv, si = lax.top_k(s, PEER_TOPK)
        cand = (sv[:, :, 0, :, None] + sv[:, :, 1, None, :]).reshape(PEER_CHUNK, PEER_HEADS, ncand)
        cidx = (si[:, :, 0, :, None] * N_KEYS + si[:, :, 1, None, :]).reshape(PEER_CHUNK, PEER_HEADS, ncand)
        fv, fi = lax.top_k(cand, PEER_TOPK)
        eidx = jnp.take_along_axis(cidx, fi, axis=-1)
        g = jax.nn.softmax(fv, axis=-1)
        act = jax.nn.gelu(jnp.einsum("td,trkd->trk", xc, u_tab[eidx]).astype(jnp.float32))
        return jnp.einsum("trk,trkd->td", (g * act).astype(xc.dtype), v_tab[eidx])

    return lax.map(chunk, xs).reshape(nc * PEER_CHUNK, D_MODEL)[:ntok]


def channel_ple(h, p, g2, wq, sub_keys, u_tab, v_tab, g3, w_gate, w_ple):
    b, L, _ = h.shape
    n = rmsnorm(h, g2).reshape(b * L, D_MODEL)
    h = h + peer_ffn(n, wq, sub_keys, u_tab, v_tab).reshape(b, L, D_MODEL)
    gate = jax.nn.sigmoid((rmsnorm(h, g3) @ w_gate).astype(jnp.float32)).astype(h.dtype)
    return h + gate * (p @ w_ple)


def setup_inputs(seed: int = 0) -> dict:
    key = jax.random.key(seed)
    ks = jax.random.split(key, 27)
    f32 = jnp.float32
    n_pages = PAST_LEN // PAGE_SIZE
    n_used = DEC_BATCH * n_pages
    n_pool = n_used + (n_used + 3) // 4
    win_len = min(WINDOW, PAST_LEN)

    def nrm(k, shape, scale):
        return jax.random.normal(k, shape, f32) * scale

    def gain(k, shape):
        return 1.0 + 0.05 * jax.random.normal(k, shape, f32)

    page_table = jax.random.permutation(ks[7], n_pool)[:n_used].reshape(DEC_BATCH, n_pages).astype(jnp.int32)
    return {
        "x_prompt": nrm(ks[0], (BATCH, SEQ, D_MODEL), 1.0),
        "x_sample": nrm(ks[1], (DEC_BATCH, DEC_SEQ, D_MODEL), 1.0),
        "p_prompt": nrm(ks[2], (DEPTH, BATCH, SEQ, PLE_DIM), 1.0),
        "p_sample": nrm(ks[3], (DEPTH, DEC_BATCH, DEC_SEQ, PLE_DIM), 1.0),
        "cache_kv": nrm(ks[4], (DEPTH, n_pool, PAGE_SIZE, N_SLOTS, N_KV, HEAD_DIM), 1.0),
        "state_win": nrm(ks[5], (DEPTH, DEC_BATCH, win_len, 2, N_KV, HEAD_DIM), 1.0),
        "state_conv": nrm(ks[6], (DEPTH, DEC_BATCH, CONV_WIDTH - 1, CONV_DIM), 1.0),
        "page_table": page_table,
        "rel_bias": nrm(ks[8], (N_BUCKETS, N_HEADS), 0.5),
        "norm1": gain(ks[9], (DEPTH, D_MODEL)),
        "w_in": nrm(ks[10], (DEPTH, D_MODEL, IN_COLS), D_MODEL ** -0.5),
        "qk_gain": gain(ks[11], (DEPTH, 4, HEAD_DIM)),
        "cmp_pe": nrm(ks[12], (DEPTH, 2, CMP_BLOCK, HEAD_DIM), 0.5),
        "cmp_w1": nrm(ks[13], (DEPTH, 2, CMP_BLOCK * HEAD_DIM, CMP_HID), (CMP_BLOCK * HEAD_DIM) ** -0.5),
        "cmp_w2": nrm(ks[14], (DEPTH, 2, CMP_HID, HEAD_DIM), CMP_HID ** -0.5),
        "conv_w": nrm(ks[15], (DEPTH, CONV_WIDTH, CONV_DIM), CONV_WIDTH ** -0.5),
        "conv_b": nrm(ks[16], (DEPTH, CONV_DIM), 0.02),
        "out_gain": gain(ks[17], (DEPTH, MIX_DIM)),
        "w_out": nrm(ks[18], (DEPTH, MIX_DIM, D_MODEL), MIX_DIM ** -0.5),
        "norm2": gain(ks[19], (DEPTH, D_MODEL)),
        "peer_wq": nrm(ks[20], (DEPTH, D_MODEL, PEER_HEADS * PEER_DK), D_MODEL ** -0.5),
        "peer_keys": nrm(ks[21], (DEPTH, 2, N_KEYS, PEER_DK // 2), (PEER_DK // 2) ** -0.5),
        "peer_u": nrm(ks[22], (DEPTH, N_EXPERTS, D_MODEL), D_MODEL ** -0.5),
        "peer_v": nrm(ks[23], (DEPTH, N_EXPERTS, D_MODEL), 0.3),
        "norm3": gain(ks[24], (DEPTH, D_MODEL)),
        "ple_gate": nrm(ks[25], (DEPTH, D_MODEL, D_MODEL), D_MODEL ** -0.5),
        "ple_proj": nrm(ks[26], (DEPTH, PLE_DIM, D_MODEL), PLE_DIM ** -0.5),
    }


def reference(x_prompt, x_sample, p_prompt, p_sample, cache_kv, state_win, state_conv, page_table,
              rel_bias, norm1, w_in, qk_gain, cmp_pe, cmp_w1, cmp_w2, conv_w, conv_b, out_gain, w_out,
              norm2, peer_wq, peer_keys, peer_u, peer_v, norm3, ple_gate, ple_proj):
    h_p, h_s = x_prompt, x_sample
    kv_p, win_p, conv_p, kv_s, win_s, conv_s = [], [], [], [], [], []
    n_dec = x_sample.shape[0]
    past_len = page_table.shape[1] * cache_kv.shape[2]
    win_len = state_win.shape[2]
    for l in range(DEPTH):
        b, L = h_p.shape[:2]
        q, rows, wrows, gates, bg, u = mix_inputs(h_p, norm1[l], w_in[l], qk_gain[l])
        keys = nsa_keys(rows, cmp_pe[l], cmp_w1[l], cmp_w2[l], qk_gain[l, 1])
        attn = nsa_prompt(q, gates, keys, wrows, rel_bias)
        prev0 = jnp.zeros((b, CONV_WIDTH - 1, CONV_DIM), u.dtype)
        conv, cst = short_conv(u, bg, prev0, conv_w[l], conv_b[l])
        h = h_p + merge_out(attn, conv, out_gain[l], w_out[l])
        h_p = channel_ple(h, p_prompt[l], norm2[l], peer_wq[l], peer_keys[l], peer_u[l], peer_v[l],
                          norm3[l], ple_gate[l], ple_proj[l])
        kv_p.append(rows)
        win_p.append(wrows[:, L - min(WINDOW, L):])
        conv_p.append(cst)

        Ls = h_s.shape[1]
        q, rows, wrows, gates, bg, u = mix_inputs(h_s, norm1[l], w_in[l], qk_gain[l])
        past = cache_kv[l][page_table].reshape(n_dec, past_len, N_SLOTS, N_KV, HEAD_DIM)
        keys = nsa_keys(jnp.concatenate([past, rows], axis=1), cmp_pe[l], cmp_w1[l], cmp_w2[l], qk_gain[l, 1])
        wctx = jnp.concatenate([state_win[l], wrows], axis=1)
        q_pos = past_len + jnp.arange(Ls)
        kw_pos = past_len - win_len + jnp.arange(win_len + Ls)
        attn = nsa_attend(q, q_pos, gates, *keys, wctx[:, :, 0], wctx[:, :, 1], kw_pos, rel_bias)
        conv, cst = short_conv(u, bg, state_conv[l], conv_w[l], conv_b[l])
        h = h_s + merge_out(attn, conv, out_gain[l], w_out[l])
        h_s = channel_ple(h, p_sample[l], norm2[l], peer_wq[l], peer_keys[l], peer_u[l], peer_v[l],
                          norm3[l], ple_gate[l], ple_proj[l])
        kv_s.append(rows)
        win_s.append(wctx[:, -win_len:])
        conv_s.append(cst)
    return (h_p, h_s, jnp.stack(kv_p), jnp.stack(win_p), jnp.stack(conv_p), jnp.stack(kv_s), jnp.stack(win_s), jnp.stack(conv_s))
```

```python
import functools
import math

import numpy as np
import jax
import jax.numpy as jnp
from jax import lax
from jax.experimental import pallas as pl
from jax.experimental.pallas import tpu as pltpu

F32 = jnp.float32
BF16 = jnp.bfloat16
I32 = jnp.int32

D_MODEL = 2048
HEAD_DIM = 128
N_HEADS = 8
N_KV = 2
HPG = N_HEADS // N_KV
ATTN_DIM = N_HEADS * HEAD_DIM
KV_DIM = N_KV * HEAD_DIM
CONV_DIM = D_MODEL - ATTN_DIM
CONV_WIDTH = 3
CMP_BLOCK = 32
CMP_STRIDE = 16
CMP_HID = 2 * HEAD_DIM
SEL_BLOCK = 64
TOP_N = 8
WINDOW = 512
Q_BLOCK = 64
N_BUCKETS = 32
MAX_DIST = 128
N_KEYS = 128
PEER_HEADS = 8
PEER_TOPK = 16
PEER_DK = 256
PLE_DIM = 256
N_BRANCH = 3
EPS = 1e-6
NEG = -1e30
FORCE = 1e4
SCALE = HEAD_DIM ** -0.5

LANES = 128
SUBLANES = 8
VMEM_LIMIT = 56 * 1024 * 1024
TOKEN_TILE = 256
PEER_TOKENS = 8
PEER_PAIRS = PEER_HEADS * PEER_TOPK
SEL_KEY_TILE = 256
KV_ROWS = 4 * KV_DIM // LANES


def _cparams(*sem):
    return pltpu.CompilerParams(dimension_semantics=sem, vmem_limit_bytes=VMEM_LIMIT)


def _const_spec(shape):
    nd = len(shape)
    return pl.BlockSpec(shape, lambda *_: (0,) * nd, pipeline_mode=pl.Buffered(1))


def _rms(x, g):
    ms = jnp.mean(x * x, axis=-1, keepdims=True)
    return x * lax.rsqrt(ms + EPS) * g


def _gelu(x):
    c = math.sqrt(2.0 / math.pi)
    return x * (0.5 * (1.0 + jnp.tanh(c * (x + 0.044715 * (x * x * x)))))


def _sigmoid(x):
    return 1.0 / (1.0 + jnp.exp(-x))


def _dot(a, b):
    return jnp.dot(a, b, preferred_element_type=F32)


def _dot_nt(a, b):
    return lax.dot_general(a, b, (((1,), (1,)), ((), ())), preferred_element_type=F32)


def _dot_split(a, b):
    hi = a.astype(BF16)
    r1 = a - hi.astype(F32)
    mid = r1.astype(BF16)
    lo = (r1 - mid.astype(F32)).astype(BF16)
    return _dot(hi, b) + _dot(mid, b) + _dot(lo, b)


def _bucket_np(dist):
    n = np.maximum(dist, 0)
    exact = N_BUCKETS // 2
    nf = np.maximum(n, 1).astype(np.float32)
    large = exact + (np.log(nf / np.float32(exact)) / np.float32(math.log(MAX_DIST / exact))
                     * np.float32(N_BUCKETS - exact)).astype(np.int32)
    return np.where(n < exact, n, np.minimum(large, N_BUCKETS - 1)).astype(np.int32)


def _bias_expand_kernel(tab_ref, idx_ref, out_ref):
    h = pl.program_id(0)
    idx = idx_ref[...]
    acc = jnp.zeros(idx.shape, F32)
    for b in range(N_BUCKETS):
        acc = jnp.where(idx == b, tab_ref[b, h], acc)
    out_ref[0] = acc


def _expand_bias(idx_np, rel_bias):
    lead = idx_np.shape[:-2]
    nq, nk = idx_np.shape[-2:]
    rows = int(np.prod(lead, dtype=np.int64)) * nq
    idx2 = jnp.asarray(idx_np.reshape(rows, nk))
    tr = rows
    while tr * nk * 4 > (1 << 20) and tr % 16 == 0:
        tr //= 2
    out = pl.pallas_call(
        _bias_expand_kernel,
        grid=(N_HEADS, rows // tr),
        in_specs=[pl.BlockSpec(memory_space=pltpu.SMEM),
                  pl.BlockSpec((tr, nk), lambda h, r: (r, 0))],
        out_specs=pl.BlockSpec((1, tr, nk), lambda h, r: (h, r, 0)),
        out_shape=jax.ShapeDtypeStruct((N_HEADS, rows, nk), F32),
        compiler_params=_cparams("arbitrary", "arbitrary"),
    )(rel_bias.astype(F32), idx2)
    nl = len(lead)
    out = out.reshape((N_KV, HPG) + lead + (nq, nk))
    perm = tuple(range(2, 2 + nl)) + (0, 1, 2 + nl, 3 + nl)
    out = out.transpose(perm)
    return out.reshape(lead + (N_KV, HPG * nq, nk))


def _head_norm(z, gain):
    ms = jnp.mean(z * z, axis=-1, keepdims=True)
    return z * lax.rsqrt(ms + EPS) * gain


def _mix_qkv_kernel(x_ref, g1_ref, w_ref, qg_ref, q_ref, rows_ref, wrows_ref, kvb_ref):
    n = _rms(x_ref[...], g1_ref[...]).astype(BF16)
    z = _dot(n, w_ref[...])
    hd = HEAD_DIM
    for h in range(N_HEADS):
        q_ref[:, h * hd:(h + 1) * hd] = _head_norm(z[:, h * hd:(h + 1) * hd], qg_ref[0:1, :]).astype(BF16)
    kv = ATTN_DIM
    rows_ref[:, 0:2 * KV_DIM] = z[:, kv:kv + 2 * KV_DIM]
    for g in range(N_KV):
        c = kv + 2 * KV_DIM + g * hd
        ks = _head_norm(z[:, c:c + hd], qg_ref[2:3, :])
        rows_ref[:, 2 * KV_DIM + g * hd:2 * KV_DIM + (g + 1) * hd] = ks
        kvb_ref[:, g * hd:(g + 1) * hd] = ks.astype(BF16)
        c = kv + 4 * KV_DIM + g * hd
        kw = _head_norm(z[:, c:c + hd], qg_ref[3:4, :])
        wrows_ref[:, g * hd:(g + 1) * hd] = kw
        kvb_ref[:, 2 * KV_DIM + g * hd:2 * KV_DIM + (g + 1) * hd] = kw.astype(BF16)
    vs = z[:, kv + 3 * KV_DIM:kv + 4 * KV_DIM]
    rows_ref[:, 3 * KV_DIM:4 * KV_DIM] = vs
    kvb_ref[:, KV_DIM:2 * KV_DIM] = vs.astype(BF16)
    vw = z[:, kv + 5 * KV_DIM:kv + 6 * KV_DIM]
    wrows_ref[:, KV_DIM:2 * KV_DIM] = vw
    kvb_ref[:, 3 * KV_DIM:4 * KV_DIM] = vw.astype(BF16)


def _mix_cv_kernel(x_ref, g1_ref, w_ref, bu_ref, gates_ref):
    n = _rms(x_ref[...], g1_ref[...]).astype(BF16)
    z = _dot(n, w_ref[...])
    c = CONV_DIM
    bu_ref[:, 0:c] = z[:, 0:c]
    bu_ref[:, c:2 * c] = z[:, c:2 * c] * z[:, 2 * c:3 * c]
    gates_ref[...] = _sigmoid(z[:, 3 * c:3 * c + LANES])


def _token_tile(n):
    return TOKEN_TILE if n % TOKEN_TILE == 0 else n


def _mix(x2, g1, w_qkv, w_cv, qk_gain):
    n = x2.shape[0]
    tm = _token_tile(n)
    grid = (n // tm,)
    row = lambda w: pl.BlockSpec((tm, w), lambda i: (i, 0))
    q, rows, wrows, kvb = pl.pallas_call(
        _mix_qkv_kernel, grid=grid,
        in_specs=[row(D_MODEL), _const_spec((1, D_MODEL)), _const_spec(w_qkv.shape), _const_spec((4, HEAD_DIM))],
        out_specs=[row(ATTN_DIM), row(4 * KV_DIM), row(2 * KV_DIM), row(4 * KV_DIM)],
        out_shape=[jax.ShapeDtypeStruct((n, ATTN_DIM), BF16), jax.ShapeDtypeStruct((n, 4 * KV_DIM), F32),
                   jax.ShapeDtypeStruct((n, 2 * KV_DIM), F32), jax.ShapeDtypeStruct((n, 4 * KV_DIM), BF16)],
        compiler_params=_cparams("arbitrary"),
    )(x2, g1, w_qkv, qk_gain)
    bu, gates = pl.pallas_call(
        _mix_cv_kernel, grid=grid,
        in_specs=[row(D_MODEL), _const_spec((1, D_MODEL)), _const_spec(w_cv.shape)],
        out_specs=[row(2 * CONV_DIM), row(LANES)],
        out_shape=[jax.ShapeDtypeStruct((n, 2 * CONV_DIM), F32), jax.ShapeDtypeStruct((n, LANES), F32)],
        compiler_params=_cparams("arbitrary"),
    )(x2, g1, w_cv)
    return q, rows, wrows, kvb, bu, gates


def _conv_kernel(bu_ref, prev_ref, w_ref, b_ref, g_ref, cn_ref, cst_ref, carry_ref):
    @pl.when(pl.program_id(1) == 0)
    def _():
        carry_ref[...] = prev_ref[0]

    c = CONV_DIM
    bg = bu_ref[0, :, 0:c]
    u = bu_ref[0, :, c:2 * c]
    tl = u.shape[0]
    row = lax.broadcasted_iota(I32, u.shape, 0)
    c0 = carry_ref[0:1, :]
    c1 = carry_ref[1:2, :]
    u1 = jnp.where(row == 0, c1, pltpu.roll(u, 1, 0))
    u2 = jnp.where(row == 0, c0, jnp.where(row == 1, c1, pltpu.roll(u, 2, 0)))
    y = w_ref[0:1, :] * u2 + w_ref[1:2, :] * u1 + w_ref[2:3, :] * u
    conv = bg * (y + b_ref[...])
    cn_ref[0] = _rms(conv, g_ref[...]).astype(BF16)
    last = u[tl - 2:tl, :]
    carry_ref[...] = last
    cst_ref[0] = last


def _conv(bu, prev, conv_w, conv_b, gain_c):
    b, l, _ = bu.shape
    tl = 512 if l % 512 == 0 else l
    c = CONV_DIM
    return pl.pallas_call(
        _conv_kernel, grid=(b, l // tl),
        in_specs=[pl.BlockSpec((1, tl, 2 * c), lambda i, t: (i, t, 0)),
                  pl.BlockSpec((1, CONV_WIDTH - 1, c), lambda i, t: (i, 0, 0)),
                  _const_spec((CONV_WIDTH, c)), _const_spec((1, c)), _const_spec((1, c))],
        out_specs=[pl.BlockSpec((1, tl, c), lambda i, t: (i, t, 0)),
                   pl.BlockSpec((1, CONV_WIDTH - 1, c), lambda i, t: (i, 0, 0))],
        out_shape=[jax.ShapeDtypeStruct((b, l, c), BF16), jax.ShapeDtypeStruct((b, CONV_WIDTH - 1, c), F32)],
        scratch_shapes=[pltpu.VMEM((CONV_WIDTH - 1, c), F32)],
        compiler_params=_cparams("arbitrary", "arbitrary"),
    )(bu, prev, conv_w, conv_b, gain_c)


def _pe_hid_kernel(pe_ref, w_ref, out_ref):
    out_ref[0] = _dot(pe_ref[0].astype(BF16), w_ref[0].astype(BF16))


def _pe_hid(cmp_pe, cmp_w1):
    k = CMP_BLOCK * HEAD_DIM
    pe = jnp.broadcast_to(cmp_pe.reshape(2, 1, k), (2, SUBLANES, k))
    return pl.pallas_call(
        _pe_hid_kernel, grid=(2,),
        in_specs=[pl.BlockSpec((1, SUBLANES, k), lambda i: (i, 0, 0)),
                  pl.BlockSpec((1, k, CMP_HID), lambda i: (i, 0, 0))],
        out_specs=pl.BlockSpec((1, SUBLANES, CMP_HID), lambda i: (i, 0, 0)),
        out_shape=jax.ShapeDtypeStruct((2, SUBLANES, CMP_HID), F32),
        compiler_params=_cparams("arbitrary"),
    )(pe, cmp_w1)


def _compress_one(slabs, w1r, pe_hid, w2):
    a = jnp.concatenate(slabs, axis=1)
    nch = a.shape[0]
    part = _dot(a, w1r)
    p0 = part[:, 0:CMP_HID]
    p1 = part[:, CMP_HID:2 * CMP_HID]
    hid = pe_hid + p0 + pltpu.roll(p1, nch - 1, 0)
    return _dot(_gelu(hid).astype(BF16), w2)


def _compress_prompt_kernel(rows_ref, w1r_ref, pe_ref, w2_ref, kg_ref, kc_ref, vc_ref):
    nch = rows_ref.shape[1] // (KV_ROWS * CMP_STRIDE)
    for br in range(2):
        for g in range(N_KV):
            j = br * N_KV + g
            slabs = [rows_ref[0, pl.ds(s * KV_ROWS + j, nch, stride=KV_ROWS * CMP_STRIDE), :].astype(BF16)
                     for s in range(CMP_STRIDE)]
            out = _compress_one(slabs, w1r_ref[br], pe_ref[br, 0:1, :], w2_ref[br])
            if br == 0:
                kc_ref[0, :, g * HEAD_DIM:(g + 1) * HEAD_DIM] = _head_norm(out, kg_ref[...]).astype(BF16)
            else:
                vc_ref[0, :, g * HEAD_DIM:(g + 1) * HEAD_DIM] = out.astype(BF16)


def _compress_prompt(rows, w1r, pe_hid, w2, kc_gain):
    b, t, _ = rows.shape
    nch = t // CMP_STRIDE
    out = jax.ShapeDtypeStruct((b, nch, KV_DIM), BF16)
    rows = rows.reshape(b, t * KV_ROWS, LANES)
    return pl.pallas_call(
        _compress_prompt_kernel, grid=(b,),
        in_specs=[pl.BlockSpec((1, t * KV_ROWS, LANES), lambda i: (i, 0, 0)),
                  _const_spec(w1r.shape), _const_spec(pe_hid.shape), _const_spec(w2.shape),
                  _const_spec((1, HEAD_DIM))],
        out_specs=[pl.BlockSpec((1, nch, KV_DIM), lambda i: (i, 0, 0))] * 2,
        out_shape=[out, out],
        compiler_params=_cparams("arbitrary"),
    )(rows, w1r, pe_hid, w2, kc_gain)


def _group_q(q, g):
    return jnp.concatenate([q[:, (g * HPG + r) * HEAD_DIM:(g * HPG + r + 1) * HEAD_DIM] for r in range(HPG)], axis=0)


def _tile_rows(x):
    return jnp.concatenate([x] * HPG, axis=0)


def _masked_softmax(logits, valid):
    l = jnp.where(valid, logits, NEG)
    m = jnp.max(l, axis=-1, keepdims=True)
    p = jnp.where(valid, jnp.exp(l - m), 0.0)
    s = jnp.sum(p, axis=-1, keepdims=True)
    return p / jnp.where(s > 0.0, s, 1.0)


def _select_blocks(pr_c, inter, cur, nq, ns):
    prsum = pr_c[0:nq]
    for r in range(1, HPG):
        prsum = prsum + pr_c[r * nq:(r + 1) * nq]
    imp = _dot_split(prsum, inter)
    sid = lax.broadcasted_iota(I32, imp.shape, 1)
    imp = jnp.where((sid == 0) | (sid == cur) | (sid == cur - 1), FORCE, imp)
    imp = jnp.where(sid > cur, -jnp.inf, imp)
    cnt = jnp.zeros(imp.shape, F32)
    for j in range(ns):
        col = imp[:, j:j + 1]
        ahead = (col > imp) | ((col == imp) & (sid > j))
        cnt = cnt + jnp.where(ahead, 1.0, 0.0)
    return jnp.where(cnt < float(TOP_N), 1.0, 0.0).astype(BF16)


def _gate_out(o_c, o_s, o_w, gates, g, nq, out_ref):
    for r in range(HPG):
        h = g * HPG + r
        sl = slice(r * nq, (r + 1) * nq)
        o = (o_c[sl] * gates[:, h:h + 1] + o_s[sl] * gates[:, N_HEADS + h:N_HEADS + h + 1]
             + o_w[sl] * gates[:, 2 * N_HEADS + h:2 * N_HEADS + h + 1])
        out_ref[0, :, h * HEAD_DIM:(h + 1) * HEAD_DIM] = o


def _inter_np(ncl, ns):
    n = np.arange(LANES)[:, None]
    s = np.arange(LANES)[None, :]
    c_start = n * CMP_STRIDE
    c_end = c_start + CMP_BLOCK - 1
    s_start = s * SEL_BLOCK
    m = (c_start < s_start + SEL_BLOCK) & (c_end >= s_start) & (n < ncl) & (s < ns)
    return m.astype(np.float32)


def _expand_np(nkeys):
    s = np.arange(LANES)[:, None]
    k = np.arange(nkeys)[None, :]
    return (k // SEL_BLOCK == s).astype(np.float32)


def _attn_prompt_kernel(q_ref, gt_ref, kc_ref, vc_ref, kvs_ref, wp_ref, tc_ref, ts_ref, tw_ref,
                        inter_ref, ex_ref, out_ref, *, ncmp, ns, nwk):
    i = pl.program_id(1)
    nq = Q_BLOCK
    rq = HPG * nq
    q = q_ref[0]
    gates = gt_ref[0]
    qrow = lax.broadcasted_iota(I32, (rq, 1), 0) % nq
    qpos = i * nq + qrow
    n_tiles = (i * nq + nq - 1) // SEL_KEY_TILE + 1
    for g in range(N_KV):
        qg = _group_q(q, g)
        gl = slice(g * HEAD_DIM, (g + 1) * HEAD_DIM)
        lc = _dot_nt(qg, kc_ref[0, :, gl]) * SCALE + tc_ref[0, g]
        jc = lax.broadcasted_iota(I32, lc.shape, 1)
        valid_c = (qpos - (jc * CMP_STRIDE + CMP_BLOCK - 1) >= 0) & (jc < ncmp)
        pr_c = _masked_softmax(lc, valid_c)
        o_c = _dot(pr_c.astype(BF16), vc_ref[0, :, gl])
        sel = _select_blocks(pr_c, inter_ref[...], i, nq, ns)

        def body(kt, carry):
            m, l, acc = carry
            k0 = pl.multiple_of(kt * SEL_KEY_TILE, SEL_KEY_TILE)
            k = kvs_ref[0, pl.ds(k0, SEL_KEY_TILE), gl]
            v = kvs_ref[0, pl.ds(k0, SEL_KEY_TILE), KV_DIM + g * HEAD_DIM:KV_DIM + (g + 1) * HEAD_DIM]
            tb = jnp.minimum(i - kt * (SEL_KEY_TILE // nq), ts_ref.shape[0] - 1)
            lg = _dot_nt(qg, k) * SCALE + ts_ref[tb, g]
            picked = _tile_rows(_dot(sel, ex_ref[:, pl.ds(k0, SEL_KEY_TILE)]))
            kpos = k0 + lax.broadcasted_iota(I32, lg.shape, 1)
            valid = (picked > 0.5) & (kpos <= qpos)
            lg = jnp.where(valid, lg, NEG)
            m_new = jnp.maximum(m, jnp.max(lg, axis=-1, keepdims=True))
            p = jnp.where(valid, jnp.exp(lg - m_new), 0.0)
            alpha = jnp.exp(m - m_new)
            l = alpha * l + jnp.sum(p, axis=-1, keepdims=True)
            acc = alpha * acc + _dot(p.astype(BF16), v)
            return m_new, l, acc

        init = (jnp.full((rq, 1), NEG, F32), jnp.zeros((rq, 1), F32), jnp.zeros((rq, HEAD_DIM), F32))
        _, l_s, acc_s = lax.fori_loop(0, n_tiles, body, init)
        o_s = acc_s / l_s

        w0 = pl.multiple_of(i * nq, nq)
        kw = wp_ref[0, pl.ds(w0, nwk), gl]
        vw = wp_ref[0, pl.ds(w0, nwk), KV_DIM + g * HEAD_DIM:KV_DIM + (g + 1) * HEAD_DIM]
        lw = _dot_nt(qg, kw) * SCALE + tw_ref[g]
        jw = lax.broadcasted_iota(I32, lw.shape, 1)
        dist = WINDOW + qrow - jw
        valid_w = (i * nq - WINDOW + jw >= 0) & (dist >= 0) & (dist < WINDOW)
        o_w = _dot(_masked_softmax(lw, valid_w).astype(BF16), vw)
        _gate_out(o_c, o_s, o_w, gates, g, nq, out_ref)


def _attn_prompt(q, gates, kc, vc, kvb, rel_bias):
    b, l, _ = q.shape
    nq = Q_BLOCK
    nblk = l // nq
    ncl = kc.shape[1]
    ncmp = ncl - 1
    ns = max(-(-l // SEL_BLOCK), TOP_N)
    nwk = WINDOW + nq + (LANES - nq)
    qi = np.arange(nq)
    blk = np.arange(nblk)
    jc = np.arange(ncl)
    idx_c = _bucket_np(blk[:, None, None] * nq + qi[None, :, None] - (jc[None, None, :] * CMP_STRIDE + CMP_BLOCK - 1))
    n_ts = min(nblk, (MAX_DIST + SEL_KEY_TILE) // nq + 1)
    mt = np.arange(n_ts)
    kj = np.arange(SEL_KEY_TILE)
    idx_s = _bucket_np(mt[:, None, None] * nq + qi[None, :, None] - kj[None, None, :])
    jw = np.arange(nwk)
    idx_w = _bucket_np(WINDOW + qi[:, None] - jw[None, :])
    tc = _expand_bias(idx_c, rel_bias)
    ts = _expand_bias(idx_s, rel_bias)
    tw = _expand_bias(idx_w, rel_bias)
    inter = jnp.asarray(_inter_np(ncmp, ns)[:ncl], BF16)
    lk = -(-l // SEL_KEY_TILE) * SEL_KEY_TILE
    ex = jnp.asarray(_expand_np(lk), BF16)
    kvs = kvb[:, :, 0:2 * KV_DIM]
    if lk != l:
        kvs = jnp.pad(kvs, ((0, 0), (0, lk - l), (0, 0)))
    wp = jnp.pad(kvb[:, :, 2 * KV_DIM:4 * KV_DIM], ((0, 0), (WINDOW, nwk - WINDOW - nq), (0, 0)))
    lw = wp.shape[1]
    kern = functools.partial(_attn_prompt_kernel, ncmp=ncmp, ns=ns, nwk=nwk)
    return pl.pallas_call(
        kern, grid=(b, nblk),
        in_specs=[pl.BlockSpec((1, nq, ATTN_DIM), lambda bi, i: (bi, i, 0)),
                  pl.BlockSpec((1, nq, LANES), lambda bi, i: (bi, i, 0)),
                  pl.BlockSpec((1, ncl, KV_DIM), lambda bi, i: (bi, 0, 0)),
                  pl.BlockSpec((1, ncl, KV_DIM), lambda bi, i: (bi, 0, 0)),
                  pl.BlockSpec((1, lk, 2 * KV_DIM), lambda bi, i: (bi, 0, 0)),
                  pl.BlockSpec((1, lw, 2 * KV_DIM), lambda bi, i: (bi, 0, 0)),
                  pl.BlockSpec((1, N_KV, HPG * nq, ncl), lambda bi, i: (i, 0, 0, 0)),
                  _const_spec(ts.shape), _const_spec(tw.shape), _const_spec(inter.shape), _const_spec(ex.shape)],
        out_specs=pl.BlockSpec((1, nq, ATTN_DIM), lambda bi, i: (bi, i, 0)),
        out_shape=jax.ShapeDtypeStruct((b, l, ATTN_DIM), F32),
        compiler_params=_cparams("arbitrary", "arbitrary"),
    )(q, gates, kc, vc, kvs, wp, tc, ts, tw, inter, ex)


def _attn_sample_kernel(pt_ref, *refs, n_pages, ns, nq, win_len):
    pages = refs[:n_pages]
    (rows_ref, wrows_ref, state_ref, q_ref, gt_ref, w1r_ref, pe_ref, w2_ref, kg_ref,
     tc_ref, ts_ref, tn_ref, tw_ref, inter_ref, ex_ref, out_ref, win_ref) = refs[n_pages:]
    del pt_ref
    page = pages[0].shape[1] // KV_ROWS
    cpp = page // CMP_STRIDE

    def page_rows(p, slot, g):
        return pages[p][0, pl.ds(slot * N_KV + g, page, stride=KV_ROWS), :].astype(BF16)
    past = n_pages * page
    ncl = past // CMP_STRIDE
    rq = HPG * nq
    q = q_ref[0]
    gates = gt_ref[0]
    qrow = lax.broadcasted_iota(I32, (rq, 1), 0) % nq
    cur = past // SEL_BLOCK
    pad_new = jnp.zeros((LANES - nq, HEAD_DIM), BF16)
    lane_new = lax.broadcasted_iota(I32, (rq, LANES), 1)
    valid_new = lane_new <= qrow

    win_ref[0, 0:win_len - nq, :] = state_ref[0, nq:win_len, :]
    win_ref[0, win_len - nq:win_len, :] = wrows_ref[0]

    for g in range(N_KV):
        qg = _group_q(q, g)
        gl = slice(g * HEAD_DIM, (g + 1) * HEAD_DIM)

        def cmp_branch(br):
            j = br * N_KV + g
            slabs = [jnp.concatenate([pages[p][0, pl.ds(s * KV_ROWS + j, cpp, stride=KV_ROWS * CMP_STRIDE), :]
                                      for p in range(n_pages)], axis=0).astype(BF16)
                     for s in range(CMP_STRIDE)]
            return _compress_one(slabs, w1r_ref[br], pe_ref[br, 0:1, :], w2_ref[br])

        kc = _head_norm(cmp_branch(0), kg_ref[...]).astype(BF16)
        vc = cmp_branch(1).astype(BF16)
        lc = _dot_nt(qg, kc) * SCALE + tc_ref[g]
        jc = lax.broadcasted_iota(I32, lc.shape, 1)
        pr_c = _masked_softmax(lc, jc < ncl - 1)
        o_c = _dot(pr_c.astype(BF16), vc)
        sel = _select_blocks(pr_c, inter_ref[...], cur, nq, ns)

        c_k = 2 * KV_DIM + g * HEAD_DIM
        c_v = 3 * KV_DIM + g * HEAD_DIM
        picked = _tile_rows(_dot(sel, ex_ref[...]))
        lg = jnp.concatenate([_dot_nt(qg, page_rows(p, 2, g)) for p in range(n_pages)], axis=1) * SCALE + ts_ref[g]
        valid = picked > 0.5
        lg = jnp.where(valid, lg, NEG)
        k_new = jnp.concatenate([rows_ref[0, :, c_k:c_k + HEAD_DIM].astype(BF16), pad_new], axis=0)
        v_new = jnp.concatenate([rows_ref[0, :, c_v:c_v + HEAD_DIM].astype(BF16), pad_new], axis=0)
        ln = jnp.where(valid_new, _dot_nt(qg, k_new) * SCALE + tn_ref[g], NEG)
        m = jnp.maximum(jnp.max(lg, axis=-1, keepdims=True), jnp.max(ln, axis=-1, keepdims=True))
        p_old = jnp.where(valid, jnp.exp(lg - m), 0.0)
        p_new = jnp.where(valid_new, jnp.exp(ln - m), 0.0)
        den = jnp.sum(p_old, axis=-1, keepdims=True) + jnp.sum(p_new, axis=-1, keepdims=True)
        p_old = (p_old / den).astype(BF16)
        o_s = _dot((p_new / den).astype(BF16), v_new)
        for p in range(n_pages):
            o_s = o_s + _dot(p_old[:, p * page:(p + 1) * page], page_rows(p, 3, g))

        kw = state_ref[0, :, gl].astype(BF16)
        vw = state_ref[0, :, KV_DIM + g * HEAD_DIM:KV_DIM + (g + 1) * HEAD_DIM].astype(BF16)
        lw = _dot_nt(qg, kw) * SCALE + tw_ref[g]
        jw = lax.broadcasted_iota(I32, lw.shape, 1)
        valid_w = jw > qrow + (win_len - WINDOW)
        lw = jnp.where(valid_w, lw, NEG)
        kwn = jnp.concatenate([wrows_ref[0, :, gl].astype(BF16), pad_new], axis=0)
        vwn = jnp.concatenate([wrows_ref[0, :, KV_DIM + g * HEAD_DIM:KV_DIM + (g + 1) * HEAD_DIM].astype(BF16),
                               pad_new], axis=0)
        lwn = jnp.where(valid_new, _dot_nt(qg, kwn) * SCALE + tn_ref[g], NEG)
        m = jnp.maximum(jnp.max(lw, axis=-1, keepdims=True), jnp.max(lwn, axis=-1, keepdims=True))
        pw_old = jnp.where(valid_w, jnp.exp(lw - m), 0.0)
        pw_new = jnp.where(valid_new, jnp.exp(lwn - m), 0.0)
        den = jnp.sum(pw_old, axis=-1, keepdims=True) + jnp.sum(pw_new, axis=-1, keepdims=True)
        o_w = _dot((pw_old / den).astype(BF16), vw) + _dot((pw_new / den).astype(BF16), vwn)
        _gate_out(o_c, o_s, o_w, gates, g, nq, out_ref)


def _attn_sample(q, gates, rows, wrows, cache, state, page_table, w1r, pe_hid, w2, kc_gain, rel_bias):
    nb, nq, _ = q.shape
    n_pages = page_table.shape[1]
    page = cache.shape[1] // KV_ROWS
    past = n_pages * page
    win_len = state.shape[1]
    t_all = past + nq
    ns = max(-(-t_all // SEL_BLOCK), TOP_N)
    ncl = past // CMP_STRIDE
    qi = np.arange(nq)
    idx_c = _bucket_np(past + qi[:, None] - (np.arange(ncl)[None, :] * CMP_STRIDE + CMP_BLOCK - 1))
    idx_s = _bucket_np(past + qi[:, None] - np.arange(past)[None, :])
    idx_n = _bucket_np(qi[:, None] - np.arange(LANES)[None, :])
    idx_w = _bucket_np(win_len + qi[:, None] - np.arange(win_len)[None, :])
    tc = _expand_bias(idx_c, rel_bias)
    ts = _expand_bias(idx_s, rel_bias)
    tn = _expand_bias(idx_n, rel_bias)
    tw = _expand_bias(idx_w, rel_bias)
    inter = jnp.asarray(_inter_np(ncl - 1, ns)[:ncl], BF16)
    ex = jnp.asarray(_expand_np(past), BF16)
    kern = functools.partial(_attn_sample_kernel, n_pages=n_pages, ns=ns, nq=nq, win_len=win_len)
    page_specs = [pl.BlockSpec((1,) + cache.shape[1:], functools.partial(lambda bi, pt, k: (pt[bi, k], 0, 0), k=k))
                  for k in range(n_pages)]
    per_b = lambda shape: pl.BlockSpec((1,) + shape, lambda bi, pt: (bi, 0, 0))
    cst = lambda a: pl.BlockSpec(a.shape, lambda bi, pt: (0,) * a.ndim, pipeline_mode=pl.Buffered(1))
    grid_spec = pltpu.PrefetchScalarGridSpec(
        num_scalar_prefetch=1, grid=(nb,),
        in_specs=page_specs + [per_b((nq, 4 * KV_DIM)), per_b((nq, 2 * KV_DIM)), per_b((win_len, 2 * KV_DIM)),
                               per_b((nq, ATTN_DIM)), per_b((nq, LANES)),
                               cst(w1r), cst(pe_hid), cst(w2), cst(kc_gain),
                               cst(tc), cst(ts), cst(tn), cst(tw), cst(inter), cst(ex)],
        out_specs=[per_b((nq, ATTN_DIM)), per_b((win_len, 2 * KV_DIM))])
    return pl.pallas_call(
        kern, grid_spec=grid_spec,
        out_shape=[jax.ShapeDtypeStruct((nb, nq, ATTN_DIM), F32),
                   jax.ShapeDtypeStruct((nb, win_len, 2 * KV_DIM), F32)],
        compiler_params=_cparams("arbitrary"),
    )(page_table, *([cache] * n_pages), rows, wrows, state, q, gates, w1r, pe_hid, w2, kc_gain,
      tc, ts, tn, tw, inter, ex)


def _merge_kernel(attn_ref, cn_ref, x_ref, ga_ref, w_ref, h_ref):
    a = _rms(attn_ref[...], ga_ref[...]).astype(BF16)
    ac = jnp.concatenate([a, cn_ref[...]], axis=1)
    h_ref[...] = x_ref[...] + _dot(ac, w_ref[...])


def _merge(attn, cn, x2, gain_a, w_out):
    n = x2.shape[0]
    tm = _token_tile(n)
    row = lambda w: pl.BlockSpec((tm, w), lambda i: (i, 0))
    return pl.pallas_call(
        _merge_kernel, grid=(n // tm,),
        in_specs=[row(ATTN_DIM), row(CONV_DIM), row(D_MODEL), _const_spec((1, ATTN_DIM)), _const_spec(w_out.shape)],
        out_specs=row(D_MODEL),
        out_shape=jax.ShapeDtypeStruct((n, D_MODEL), F32),
        compiler_params=_cparams("arbitrary"),
    )(attn, cn, x2, gain_a, w_out)


def _top_rows(v, k, payload=None):
    nrow = v.shape[0]
    rid = lax.broadcasted_iota(I32, v.shape, 0).astype(F32)
    vals, ids = [], []
    for _ in range(k):
        m = jnp.max(v, axis=0, keepdims=True)
        am = jnp.min(jnp.where(v == m, rid, float(nrow)), axis=0, keepdims=True)
        hit = rid == am
        vals.append(m)
        if payload is None:
            ids.append(am)
        else:
            ids.append(jnp.max(jnp.where(hit, payload, -1.0), axis=0, keepdims=True))
        v = jnp.where(hit, -jnp.inf, v)
    return jnp.concatenate(vals, axis=0), jnp.concatenate(ids, axis=0)


def _route_kernel(h_ref, g2_ref, wq_ref, keys_ref, n2_ref, eidx_ref, gw_ref, qh_ref):
    n2 = _rms(h_ref[...], g2_ref[...])
    n2_ref[...] = n2
    qh_ref[...] = _dot(n2.astype(BF16), wq_ref[...])
    half = PEER_DK // 2

    def head(r, carry):
        sv, si = [], []
        for hf in range(2):
            c0 = pl.multiple_of(r * PEER_DK + hf * half, half)
            qc = qh_ref[:, pl.ds(c0, half)].astype(BF16)
            st = _dot_nt(keys_ref[hf], qc)
            v, ix = _top_rows(st, PEER_TOPK)
            sv.append(v)
            si.append(ix)
        cand = jnp.concatenate([sv[0][a:a + 1] + sv[1] for a in range(PEER_TOPK)], axis=0)
        cidx = jnp.concatenate([si[0][a:a + 1] * float(N_KEYS) + si[1] for a in range(PEER_TOPK)], axis=0)
        fv, fe = _top_rows(cand, PEER_TOPK, payload=cidx)
        e = jnp.exp(fv - fv[0:1])
        gw_ref[r] = e / jnp.sum(e, axis=0, keepdims=True)
        eidx_ref[r] = fe.astype(I32)
        return carry

    lax.fori_loop(0, PEER_HEADS, head, 0)


def _route(h, g2, wq, keys):
    n = h.shape[0]
    tm = _token_tile(n)
    row = lambda w: pl.BlockSpec((tm, w), lambda i: (i, 0))
    hk = pl.BlockSpec((PEER_HEADS, PEER_TOPK, tm), lambda i: (0, 0, i))
    return pl.pallas_call(
        _route_kernel, grid=(n // tm,),
        in_specs=[row(D_MODEL), _const_spec((1, D_MODEL)), _const_spec(wq.shape), _const_spec(keys.shape)],
        out_specs=[row(D_MODEL), hk, hk],
        out_shape=[jax.ShapeDtypeStruct((n, D_MODEL), F32),
                   jax.ShapeDtypeStruct((PEER_HEADS, PEER_TOPK, n), I32),
                   jax.ShapeDtypeStruct((PEER_HEADS, PEER_TOPK, n), F32)],
        scratch_shapes=[pltpu.VMEM((tm, PEER_HEADS * PEER_DK), F32)],
        compiler_params=_cparams("arbitrary"),
    )(h, g2, wq, keys)


_BITREV = (0, 4, 2, 6, 1, 5, 3, 7)


def _sublane_sums(vs):
    sub = lax.broadcasted_iota(I32, (SUBLANES, LANES), 0)
    cur = [vs[j] for j in _BITREV]
    sh = SUBLANES // 2
    while sh >= 1:
        low = (sub & sh) == 0
        nxt = []
        for j in range(0, len(cur), 2):
            a, b = cur[j], cur[j + 1]
            nxt.append(jnp.where(low, a, pltpu.roll(b, sh, 0)) + jnp.where(low, pltpu.roll(a, SUBLANES - sh, 0), b))
        cur = nxt
        sh //= 2
    return cur[0]


def _peer_kernel(idx_cur_ref, idx_nxt_ref, x_ref, h_ref, gw_ref, tab_ref, out_ref, buf_ref, sem_ref, *, rows_per_tok):
    i = pl.program_id(0)
    nsteps = pl.num_programs(0)
    tt = x_ref.shape[0]
    nrow = tt * PEER_PAIRS
    slot = i % 2
    half = rows_per_tok // 2

    def issue(idx_ref, dst_slot):
        def chunk(c, carry):
            for k in range(SUBLANES):
                j = c * SUBLANES + k
                e = idx_ref[0, 0, j]
                pltpu.make_async_copy(tab_ref.at[e], buf_ref.at[dst_slot, j], sem_ref.at[dst_slot]).start()
            return carry
        lax.fori_loop(0, nrow // SUBLANES, chunk, 0)

    @pl.when(i == 0)
    def _():
        issue(idx_cur_ref, 0)

    @pl.when(i + 1 < nsteps)
    def _():
        issue(idx_nxt_ref, 1 - slot)

    pltpu.make_async_copy(tab_ref.at[pl.ds(0, nrow)], buf_ref.at[slot], sem_ref.at[slot]).wait()

    lane = lax.broadcasted_iota(I32, gw_ref.shape, 1)
    tok0 = (i * tt) % gw_ref.shape[1]

    def token(t, carry):
        x = x_ref[t]
        x_lo = x[0:SUBLANES]
        x_hi = x[SUBLANES:2 * SUBLANES]
        gcol = jnp.sum(jnp.where(lane == tok0 + t, gw_ref[...], 0.0), axis=-1, keepdims=True)
        acc_lo = jnp.zeros((SUBLANES, LANES), F32)
        acc_hi = jnp.zeros((SUBLANES, LANES), F32)
        for grp in range(PEER_PAIRS // SUBLANES):
            base = t * PEER_PAIRS + grp * SUBLANES
            prods = []
            for k in range(SUBLANES):
                u = buf_ref[slot, base + k, 0:half, :]
                prods.append(u[0:SUBLANES] * x_lo + u[SUBLANES:2 * SUBLANES] * x_hi)
            act = jnp.sum(_sublane_sums(prods), axis=-1, keepdims=True)
            coef = _gelu(act) * gcol[grp * SUBLANES:(grp + 1) * SUBLANES]
            coef = jnp.broadcast_to(coef, (SUBLANES, LANES))
            for k in range(SUBLANES):
                v = buf_ref[slot, base + k, half:2 * half, :]
                ck = jnp.broadcast_to(coef[k:k + 1, :], (SUBLANES, LANES))
                acc_lo = acc_lo + ck * v[0:SUBLANES]
                acc_hi = acc_hi + ck * v[SUBLANES:2 * SUBLANES]
        out_ref[t] = h_ref[t] + jnp.concatenate([acc_lo, acc_hi], axis=0)
        return carry

    lax.fori_loop(0, tt, token, 0)


def _peer(n2, h, eidx, gw, tab):
    n = n2.shape[0]
    tt = PEER_TOKENS
    nsteps = n // tt
    sub = D_MODEL // LANES
    x3 = n2.reshape(n, sub, LANES)
    h3 = h.reshape(n, sub, LANES)
    idx3 = eidx.reshape(nsteps, 1, tt * PEER_PAIRS)
    rows_per_tok = tab.shape[1]
    tok = pl.BlockSpec((tt, sub, LANES), lambda i: (i, 0, 0))
    smem = lambda f: pl.BlockSpec((1, 1, tt * PEER_PAIRS), f, memory_space=pltpu.SMEM)
    gw_lanes = LANES if n % LANES == 0 else n
    out = pl.pallas_call(
        functools.partial(_peer_kernel, rows_per_tok=rows_per_tok), grid=(nsteps,),
        in_specs=[smem(lambda i: (i, 0, 0)), smem(lambda i: (jnp.minimum(i + 1, nsteps - 1), 0, 0)),
                  tok, tok,
                  pl.BlockSpec((PEER_PAIRS, gw_lanes), lambda i: (0, (i * tt) // gw_lanes)),
                  pl.BlockSpec(memory_space=pl.ANY)],
        out_specs=tok,
        out_shape=jax.ShapeDtypeStruct((n, sub, LANES), F32),
        scratch_shapes=[pltpu.VMEM((2, tt * PEER_PAIRS, rows_per_tok, LANES), F32),
                        pltpu.SemaphoreType.DMA((2,))],
        compiler_params=_cparams("arbitrary"),
    )(idx3, idx3, x3, h3, gw, tab)
    return out.reshape(n, D_MODEL)


def _ple_kernel(h_ref, p_ref, g3_ref, wg_ref, wp_ref, y_ref):
    h = h_ref[...]
    gate = _sigmoid(_dot(_rms(h, g3_ref[...]).astype(BF16), wg_ref[...]))
    y_ref[...] = h + gate * _dot(p_ref[...].astype(BF16), wp_ref[...])


def _ple(h, p, g3, w_gate, w_ple):
    n = h.shape[0]
    tm = _token_tile(n)
    row = lambda w: pl.BlockSpec((tm, w), lambda i: (i, 0))
    return pl.pallas_call(
        _ple_kernel, grid=(n // tm,),
        in_specs=[row(D_MODEL), row(PLE_DIM), _const_spec((1, D_MODEL)), _const_spec(w_gate.shape),
                  _const_spec(w_ple.shape)],
        out_specs=row(D_MODEL),
        out_shape=jax.ShapeDtypeStruct((n, D_MODEL), F32),
        compiler_params=_cparams("arbitrary"),
    )(h, p, g3, w_gate, w_ple)


def _tail(attn, cn, x2, p2, wts):
    h = _merge(attn, cn, x2, wts["gain_a"], wts["w_out"])
    n2, eidx_t, gw_t = _route(h, wts["g2"], wts["wq"], wts["keys"])
    n = h.shape[0]
    eidx = eidx_t.reshape(PEER_PAIRS, n).T
    gw = gw_t.reshape(PEER_PAIRS, n)
    h = _peer(n2, h, eidx, gw, wts["tab"])
    return _ple(h, p2, wts["g3"], wts["w_gate"], wts["w_ple"])


def kernel(x_prompt, x_sample, p_prompt, p_sample, cache_kv, state_win, state_conv, page_table, rel_bias, norm1,
           w_in, qk_gain, cmp_pe, cmp_w1, cmp_w2, conv_w, conv_b, out_gain, w_out, norm2, peer_wq, peer_keys,
           peer_u, peer_v, norm3, ple_gate, ple_proj):
    depth = norm1.shape[0]
    assert depth == 1
    l0 = 0
    b, l, d = x_prompt.shape
    nb, nq, _ = x_sample.shape
    n_pool, page = cache_kv.shape[1], cache_kv.shape[2]
    win_len = state_win.shape[2]

    w = w_in[l0]
    c_gt = ATTN_DIM + 6 * KV_DIM
    c_cv = c_gt + N_BRANCH * N_HEADS
    w_qkv = w[:, 0:c_gt].astype(BF16)
    w_cv = jnp.concatenate([w[:, c_cv:], jnp.pad(w[:, c_gt:c_cv], ((0, 0), (0, LANES - N_BRANCH * N_HEADS)))],
                           axis=1).astype(BF16)
    g1 = norm1[l0].reshape(1, d)
    w1 = cmp_w1[l0]
    half_k = CMP_STRIDE * HEAD_DIM
    w1r = jnp.concatenate([w1[:, 0:half_k], w1[:, half_k:2 * half_k]], axis=2).astype(BF16)
    w2 = cmp_w2[l0].astype(BF16)
    pe_hid = _pe_hid(cmp_pe[l0], w1)
    kc_gain = qk_gain[l0, 1].reshape(1, HEAD_DIM)
    wts = dict(
        gain_a=out_gain[l0, 0:ATTN_DIM].reshape(1, ATTN_DIM), w_out=w_out[l0].astype(BF16),
        g2=norm2[l0].reshape(1, d), wq=peer_wq[l0].astype(BF16), keys=peer_keys[l0].astype(BF16),
        tab=jnp.concatenate([peer_u[l0].reshape(-1, d // LANES, LANES), peer_v[l0].reshape(-1, d // LANES, LANES)],
                            axis=1),
        g3=norm3[l0].reshape(1, d), w_gate=ple_gate[l0].astype(BF16), w_ple=ple_proj[l0].astype(BF16))
    gain_c = out_gain[l0, ATTN_DIM:].reshape(1, CONV_DIM)
    cw = conv_w[l0]
    cb = conv_b[l0].reshape(1, CONV_DIM)

    xp = x_prompt.reshape(b * l, d)
    q, rows, wrows, kvb, bu, gates = _mix(xp, g1, w_qkv, w_cv, qk_gain[l0])
    rows3 = rows.reshape(b, l, 4 * KV_DIM)
    kc, vc = _compress_prompt(rows3, w1r, pe_hid, w2, kc_gain)
    attn = _attn_prompt(q.reshape(b, l, ATTN_DIM), gates.reshape(b, l, LANES), kc, vc,
                        kvb.reshape(b, l, 4 * KV_DIM), rel_bias)
    cn, cst_p = _conv(bu.reshape(b, l, 2 * CONV_DIM), jnp.zeros((b, CONV_WIDTH - 1, CONV_DIM), F32), cw, cb, gain_c)
    y_p = _tail(attn.reshape(b * l, ATTN_DIM), cn.reshape(b * l, CONV_DIM), xp,
                p_prompt[l0].reshape(b * l, PLE_DIM), wts)
    wl = min(WINDOW, l)
    kv_p = rows3.reshape(1, b, l, 4, N_KV, HEAD_DIM)
    win_p = wrows.reshape(b, l, 2, N_KV, HEAD_DIM)[None, :, l - wl:]
    conv_p = cst_p[None]

    xs = x_sample.reshape(nb * nq, d)
    q, rows, wrows, kvb, bu, gates = _mix(xs, g1, w_qkv, w_cv, qk_gain[l0])
    attn, win_s = _attn_sample(q.reshape(nb, nq, ATTN_DIM), gates.reshape(nb, nq, LANES),
                               rows.reshape(nb, nq, 4 * KV_DIM), wrows.reshape(nb, nq, 2 * KV_DIM),
                               cache_kv[l0].reshape(n_pool, page * KV_ROWS, LANES),
                               state_win[l0].reshape(nb, win_len, 2 * KV_DIM), page_table,
                               w1r, pe_hid, w2, kc_gain, rel_bias)
    cn, cst_s = _conv(bu.reshape(nb, nq, 2 * CONV_DIM), state_conv[l0], cw, cb, gain_c)
    y_s = _tail(attn.reshape(nb * nq, ATTN_DIM), cn.reshape(nb * nq, CONV_DIM), xs,
                p_sample[l0].reshape(nb * nq, PLE_DIM), wts)
    kv_s = rows.reshape(1, nb, nq, 4, N_KV, HEAD_DIM)
    win_s = win_s.reshape(1, nb, win_len, 2, N_KV, HEAD_DIM)
    conv_s = cst_s[None]

    return (y_p.reshape(b, l, d), y_s.reshape(nb, nq, d), kv_p, win_p, conv_p, kv_s, win_s, conv_s)
```

```python
import functools
import math

import numpy as np
import jax
import jax.numpy as jnp
from jax import lax
from jax.experimental import pallas as pl
from jax.experimental.pallas import tpu as pltpu

F32 = jnp.float32
BF16 = jnp.bfloat16
I32 = jnp.int32

D_MODEL = 2048
HEAD_DIM = 128
N_HEADS = 8
N_KV = 2
HPG = N_HEADS // N_KV
ATTN_DIM = N_HEADS * HEAD_DIM
KV_DIM = N_KV * HEAD_DIM
CONV_DIM = D_MODEL - ATTN_DIM
CONV_WIDTH = 3
CMP_BLOCK = 32
CMP_STRIDE = 16
CMP_HID = 2 * HEAD_DIM
SEL_BLOCK = 64
TOP_N = 8
WINDOW = 512
Q_BLOCK = 64
N_BUCKETS = 32
MAX_DIST = 128
N_KEYS = 128
PEER_HEADS = 8
PEER_TOPK = 16
PEER_DK = 256
PLE_DIM = 256
N_BRANCH = 3
EPS = 1e-6
NEG = -1e30
FORCE = 1e4
SCALE = HEAD_DIM ** -0.5

LANES = 128
SUBLANES = 8
VMEM_LIMIT = 56 * 1024 * 1024
TOKEN_TILE = 256
PEER_TOKENS = 8
PEER_PAIRS = PEER_HEADS * PEER_TOPK
SEL_KEY_TILE = 256
KV_ROWS = 4 * KV_DIM // LANES


def _cparams(*sem):
    return pltpu.CompilerParams(dimension_semantics=sem, vmem_limit_bytes=VMEM_LIMIT)


def _const_spec(shape):
    nd = len(shape)
    return pl.BlockSpec(shape, lambda *_: (0,) * nd, pipeline_mode=pl.Buffered(1))


def _rms(x, g):
    ms = jnp.mean(x * x, axis=-1, keepdims=True)
    return x * lax.rsqrt(ms + EPS) * g


def _gelu(x):
    c = math.sqrt(2.0 / math.pi)
    return x * (0.5 * (1.0 + jnp.tanh(c * (x + 0.044715 * (x * x * x)))))


def _sigmoid(x):
    return 1.0 / (1.0 + jnp.exp(-x))


def _dot(a, b):
    return jnp.dot(a, b, preferred_element_type=F32)


def _dot_nt(a, b):
    return lax.dot_general(a, b, (((1,), (1,)), ((), ())), preferred_element_type=F32)


def _dot_split(a, b):
    hi = a.astype(BF16)
    r1 = a - hi.astype(F32)
    mid = r1.astype(BF16)
    lo = (r1 - mid.astype(F32)).astype(BF16)
    return _dot(hi, b) + _dot(mid, b) + _dot(lo, b)


def _bucket_np(dist):
    n = np.maximum(dist, 0)
    exact = N_BUCKETS // 2
    nf = np.maximum(n, 1).astype(np.float32)
    large = exact + (np.log(nf / np.float32(exact)) / np.float32(math.log(MAX_DIST / exact))
                     * np.float32(N_BUCKETS - exact)).astype(np.int32)
    return np.where(n < exact, n, np.minimum(large, N_BUCKETS - 1)).astype(np.int32)


def _bias_expand_kernel(tab_ref, idx_ref, out_ref):
    h = pl.program_id(0)
    idx = idx_ref[...]
    acc = jnp.zeros(idx.shape, F32)
    for b in range(N_BUCKETS):
        acc = jnp.where(idx == b, tab_ref[b, h], acc)
    out_ref[0] = acc


def _expand_bias(idx_np, rel_bias):
    lead = idx_np.shape[:-2]
    nq, nk = idx_np.shape[-2:]
    rows = int(np.prod(lead, dtype=np.int64)) * nq
    idx2 = jnp.asarray(idx_np.reshape(rows, nk))
    tr = rows
    while tr * nk * 4 > (1 << 20) and tr % 16 == 0:
        tr //= 2
    out = pl.pallas_call(
        _bias_expand_kernel, name="bias_expand",
        grid=(N_HEADS, rows // tr),
        in_specs=[pl.BlockSpec(memory_space=pltpu.SMEM),
                  pl.BlockSpec((tr, nk), lambda h, r: (r, 0))],
        out_specs=pl.BlockSpec((1, tr, nk), lambda h, r: (h, r, 0)),
        out_shape=jax.ShapeDtypeStruct((N_HEADS, rows, nk), F32),
        compiler_params=_cparams("arbitrary", "arbitrary"),
    )(rel_bias.astype(F32), idx2)
    nl = len(lead)
    out = out.reshape((N_KV, HPG) + lead + (nq, nk))
    perm = tuple(range(2, 2 + nl)) + (0, 1, 2 + nl, 3 + nl)
    out = out.transpose(perm)
    return out.reshape(lead + (N_KV, HPG * nq, nk))


def _head_norm(z, gain):
    ms = jnp.mean(z * z, axis=-1, keepdims=True)
    return z * lax.rsqrt(ms + EPS) * gain


def _mix_qkv_kernel(x_ref, g1_ref, w_ref, qg_ref, q_ref, rows_ref, wrows_ref, kvb_ref):
    n = _rms(x_ref[...], g1_ref[...]).astype(BF16)
    z = _dot(n, w_ref[...])
    hd = HEAD_DIM
    for h in range(N_HEADS):
        q_ref[:, h * hd:(h + 1) * hd] = _head_norm(z[:, h * hd:(h + 1) * hd], qg_ref[0:1, :]).astype(BF16)
    kv = ATTN_DIM
    rows_ref[:, 0:2 * KV_DIM] = z[:, kv:kv + 2 * KV_DIM]
    for g in range(N_KV):
        c = kv + 2 * KV_DIM + g * hd
        ks = _head_norm(z[:, c:c + hd], qg_ref[2:3, :])
        rows_ref[:, 2 * KV_DIM + g * hd:2 * KV_DIM + (g + 1) * hd] = ks
        kvb_ref[:, g * hd:(g + 1) * hd] = ks.astype(BF16)
        c = kv + 4 * KV_DIM + g * hd
        kw = _head_norm(z[:, c:c + hd], qg_ref[3:4, :])
        wrows_ref[:, g * hd:(g + 1) * hd] = kw
        kvb_ref[:, 2 * KV_DIM + g * hd:2 * KV_DIM + (g + 1) * hd] = kw.astype(BF16)
    vs = z[:, kv + 3 * KV_DIM:kv + 4 * KV_DIM]
    rows_ref[:, 3 * KV_DIM:4 * KV_DIM] = vs
    kvb_ref[:, KV_DIM:2 * KV_DIM] = vs.astype(BF16)
    vw = z[:, kv + 5 * KV_DIM:kv + 6 * KV_DIM]
    wrows_ref[:, KV_DIM:2 * KV_DIM] = vw
    kvb_ref[:, 3 * KV_DIM:4 * KV_DIM] = vw.astype(BF16)


def _mix_cv_kernel(x_ref, g1_ref, w_ref, bu_ref, gates_ref):
    n = _rms(x_ref[...], g1_ref[...]).astype(BF16)
    z = _dot(n, w_ref[...])
    c = CONV_DIM
    bu_ref[:, 0:c] = z[:, 0:c]
    bu_ref[:, c:2 * c] = z[:, c:2 * c] * z[:, 2 * c:3 * c]
    gates_ref[...] = _sigmoid(z[:, 3 * c:3 * c + LANES])


def _token_tile(n):
    return TOKEN_TILE if n % TOKEN_TILE == 0 else n


def _mix(x2, g1, w_qkv, w_cv, qk_gain):
    n = x2.shape[0]
    tm = _token_tile(n)
    grid = (n // tm,)
    row = lambda w: pl.BlockSpec((tm, w), lambda i: (i, 0))
    q, rows, wrows, kvb = pl.pallas_call(
        _mix_qkv_kernel, grid=grid, name="mix_qkv",
        in_specs=[row(D_MODEL), _const_spec((1, D_MODEL)), _const_spec(w_qkv.shape), _const_spec((4, HEAD_DIM))],
        out_specs=[row(ATTN_DIM), row(4 * KV_DIM), row(2 * KV_DIM), row(4 * KV_DIM)],
        out_shape=[jax.ShapeDtypeStruct((n, ATTN_DIM), BF16), jax.ShapeDtypeStruct((n, 4 * KV_DIM), F32),
                   jax.ShapeDtypeStruct((n, 2 * KV_DIM), F32), jax.ShapeDtypeStruct((n, 4 * KV_DIM), BF16)],
        compiler_params=_cparams("arbitrary"),
    )(x2, g1, w_qkv, qk_gain)
    bu, gates = pl.pallas_call(
        _mix_cv_kernel, grid=grid, name="mix_cv",
        in_specs=[row(D_MODEL), _const_spec((1, D_MODEL)), _const_spec(w_cv.shape)],
        out_specs=[row(2 * CONV_DIM), row(LANES)],
        out_shape=[jax.ShapeDtypeStruct((n, 2 * CONV_DIM), F32), jax.ShapeDtypeStruct((n, LANES), F32)],
        compiler_params=_cparams("arbitrary"),
    )(x2, g1, w_cv)
    return q, rows, wrows, kvb, bu, gates


def _conv_kernel(bu_ref, prev_ref, w_ref, b_ref, g_ref, cn_ref, cst_ref, carry_ref):
    @pl.when(pl.program_id(1) == 0)
    def _():
        carry_ref[...] = prev_ref[0]

    c = CONV_DIM
    bg = bu_ref[0, :, 0:c]
    u = bu_ref[0, :, c:2 * c]
    tl = u.shape[0]
    row = lax.broadcasted_iota(I32, u.shape, 0)
    c0 = carry_ref[0:1, :]
    c1 = carry_ref[1:2, :]
    u1 = jnp.where(row == 0, c1, pltpu.roll(u, 1, 0))
    u2 = jnp.where(row == 0, c0, jnp.where(row == 1, c1, pltpu.roll(u, 2, 0)))
    y = w_ref[0:1, :] * u2 + w_ref[1:2, :] * u1 + w_ref[2:3, :] * u
    conv = bg * (y + b_ref[...])
    cn_ref[0] = _rms(conv, g_ref[...]).astype(BF16)
    last = u[tl - 2:tl, :]
    carry_ref[...] = last
    cst_ref[0] = last


def _conv(bu, prev, conv_w, conv_b, gain_c):
    b, l, _ = bu.shape
    tl = 512 if l % 512 == 0 else l
    c = CONV_DIM
    return pl.pallas_call(
        _conv_kernel, grid=(b, l // tl), name="short_conv",
        in_specs=[pl.BlockSpec((1, tl, 2 * c), lambda i, t: (i, t, 0)),
                  pl.BlockSpec((1, CONV_WIDTH - 1, c), lambda i, t: (i, 0, 0)),
                  _const_spec((CONV_WIDTH, c)), _const_spec((1, c)), _const_spec((1, c))],
        out_specs=[pl.BlockSpec((1, tl, c), lambda i, t: (i, t, 0)),
                   pl.BlockSpec((1, CONV_WIDTH - 1, c), lambda i, t: (i, 0, 0))],
        out_shape=[jax.ShapeDtypeStruct((b, l, c), BF16), jax.ShapeDtypeStruct((b, CONV_WIDTH - 1, c), F32)],
        scratch_shapes=[pltpu.VMEM((CONV_WIDTH - 1, c), F32)],
        compiler_params=_cparams("arbitrary", "arbitrary"),
    )(bu, prev, conv_w, conv_b, gain_c)


def _pe_hid_kernel(pe_ref, w_ref, out_ref):
    out_ref[0] = _dot(pe_ref[0].astype(BF16), w_ref[0].astype(BF16))


def _pe_hid(cmp_pe, cmp_w1):
    k = CMP_BLOCK * HEAD_DIM
    pe = jnp.broadcast_to(cmp_pe.reshape(2, 1, k), (2, SUBLANES, k))
    return pl.pallas_call(
        _pe_hid_kernel, grid=(2,), name="pe_hid",
        in_specs=[pl.BlockSpec((1, SUBLANES, k), lambda i: (i, 0, 0)),
                  pl.BlockSpec((1, k, CMP_HID), lambda i: (i, 0, 0))],
        out_specs=pl.BlockSpec((1, SUBLANES, CMP_HID), lambda i: (i, 0, 0)),
        out_shape=jax.ShapeDtypeStruct((2, SUBLANES, CMP_HID), F32),
        compiler_params=_cparams("arbitrary"),
    )(pe, cmp_w1)


def _compress_one(slabs, w1r, pe_hid, w2):
    a = jnp.concatenate(slabs, axis=1)
    nch = a.shape[0]
    part = _dot(a, w1r)
    p0 = part[:, 0:CMP_HID]
    p1 = part[:, CMP_HID:2 * CMP_HID]
    hid = pe_hid + p0 + pltpu.roll(p1, nch - 1, 0)
    return _dot(_gelu(hid).astype(BF16), w2)


def _compress_prompt_kernel(rows_ref, w1r_ref, pe_ref, w2_ref, kg_ref, kc_ref, vc_ref):
    nch = rows_ref.shape[1] // (KV_ROWS * CMP_STRIDE)
    for br in range(2):
        for g in range(N_KV):
            j = br * N_KV + g
            slabs = [rows_ref[0, pl.ds(s * KV_ROWS + j, nch, stride=KV_ROWS * CMP_STRIDE), :].astype(BF16)
                     for s in range(CMP_STRIDE)]
            out = _compress_one(slabs, w1r_ref[br], pe_ref[br, 0:1, :], w2_ref[br])
            if br == 0:
                kc_ref[0, :, g * HEAD_DIM:(g + 1) * HEAD_DIM] = _head_norm(out, kg_ref[...]).astype(BF16)
            else:
                vc_ref[0, :, g * HEAD_DIM:(g + 1) * HEAD_DIM] = out.astype(BF16)


def _compress_prompt(rows, w1r, pe_hid, w2, kc_gain):
    b, t, _ = rows.shape
    nch = t // CMP_STRIDE
    out = jax.ShapeDtypeStruct((b, nch, KV_DIM), BF16)
    rows = rows.reshape(b, t * KV_ROWS, LANES)
    return pl.pallas_call(
        _compress_prompt_kernel, grid=(b,), name="compress_prompt",
        in_specs=[pl.BlockSpec((1, t * KV_ROWS, LANES), lambda i: (i, 0, 0)),
                  _const_spec(w1r.shape), _const_spec(pe_hid.shape), _const_spec(w2.shape),
                  _const_spec((1, HEAD_DIM))],
        out_specs=[pl.BlockSpec((1, nch, KV_DIM), lambda i: (i, 0, 0))] * 2,
        out_shape=[out, out],
        compiler_params=_cparams("arbitrary"),
    )(rows, w1r, pe_hid, w2, kc_gain)


def _group_q(q, g):
    return jnp.concatenate([q[:, (g * HPG + r) * HEAD_DIM:(g * HPG + r + 1) * HEAD_DIM] for r in range(HPG)], axis=0)


def _tile_rows(x):
    return jnp.concatenate([x] * HPG, axis=0)


def _masked_softmax(logits, valid):
    l = jnp.where(valid, logits, NEG)
    m = jnp.max(l, axis=-1, keepdims=True)
    p = jnp.where(valid, jnp.exp(l - m), 0.0)
    s = jnp.sum(p, axis=-1, keepdims=True)
    return p / jnp.where(s > 0.0, s, 1.0)


def _select_blocks(pr_c, inter, cur, nq, ns):
    prsum = pr_c[0:nq]
    for r in range(1, HPG):
        prsum = prsum + pr_c[r * nq:(r + 1) * nq]
    imp = _dot_split(prsum, inter)
    sid = lax.broadcasted_iota(I32, imp.shape, 1)
    imp = jnp.where((sid == 0) | (sid == cur) | (sid == cur - 1), FORCE, imp)
    imp = jnp.where(sid > cur, -jnp.inf, imp)
    cnt = jnp.zeros(imp.shape, F32)
    for j in range(ns):
        col = imp[:, j:j + 1]
        ahead = (col > imp) | ((col == imp) & (sid > j))
        cnt = cnt + jnp.where(ahead, 1.0, 0.0)
    return jnp.where(cnt < float(TOP_N), 1.0, 0.0).astype(BF16)


def _gate_out(o_c, o_s, o_w, gates, g, nq, out_ref):
    for r in range(HPG):
        h = g * HPG + r
        sl = slice(r * nq, (r + 1) * nq)
        o = (o_c[sl] * gates[:, h:h + 1] + o_s[sl] * gates[:, N_HEADS + h:N_HEADS + h + 1]
             + o_w[sl] * gates[:, 2 * N_HEADS + h:2 * N_HEADS + h + 1])
        out_ref[0, :, h * HEAD_DIM:(h + 1) * HEAD_DIM] = o


def _inter_np(ncl, ns):
    n = np.arange(LANES)[:, None]
    s = np.arange(LANES)[None, :]
    c_start = n * CMP_STRIDE
    c_end = c_start + CMP_BLOCK - 1
    s_start = s * SEL_BLOCK
    m = (c_start < s_start + SEL_BLOCK) & (c_end >= s_start) & (n < ncl) & (s < ns)
    return m.astype(np.float32)


def _expand_np(nkeys):
    s = np.arange(LANES)[:, None]
    k = np.arange(nkeys)[None, :]
    return (k // SEL_BLOCK == s).astype(np.float32)


def _attn_prompt_kernel(q_ref, gt_ref, kc_ref, vc_ref, kvs_ref, wp_ref, tc_ref, ts_ref, tw_ref,
                        inter_ref, ex_ref, out_ref, *, ncmp, ns, nwk):
    i = pl.program_id(1)
    nq = Q_BLOCK
    rq = HPG * nq
    q = q_ref[0]
    gates = gt_ref[0]
    qrow = lax.broadcasted_iota(I32, (rq, 1), 0) % nq
    qpos = i * nq + qrow
    n_tiles = (i * nq + nq - 1) // SEL_KEY_TILE + 1
    qgs, o_cs, sels = [], [], []
    for g in range(N_KV):
        qg = _group_q(q, g)
        gl = slice(g * HEAD_DIM, (g + 1) * HEAD_DIM)
        lc = _dot_nt(qg, kc_ref[0, :, gl]) * SCALE + tc_ref[0, g]
        jc = lax.broadcasted_iota(I32, lc.shape, 1)
        valid_c = (qpos - (jc * CMP_STRIDE + CMP_BLOCK - 1) >= 0) & (jc < ncmp)
        pr_c = _masked_softmax(lc, valid_c)
        qgs.append(qg)
        o_cs.append(_dot(pr_c.astype(BF16), vc_ref[0, :, gl]))
        sels.append(_select_blocks(pr_c, inter_ref[...], i, nq, ns))

    def body(kt, carry):
        k0 = pl.multiple_of(kt * SEL_KEY_TILE, SEL_KEY_TILE)
        tb = jnp.minimum(i - kt * (SEL_KEY_TILE // nq), ts_ref.shape[0] - 1)
        kpos = k0 + lax.broadcasted_iota(I32, (rq, SEL_KEY_TILE), 1)
        causal = kpos <= qpos
        out = []
        for g in range(N_KV):
            m, l, acc = carry[g]
            k = kvs_ref[0, pl.ds(k0, SEL_KEY_TILE), g * HEAD_DIM:(g + 1) * HEAD_DIM]
            v = kvs_ref[0, pl.ds(k0, SEL_KEY_TILE), KV_DIM + g * HEAD_DIM:KV_DIM + (g + 1) * HEAD_DIM]
            lg = _dot_nt(qgs[g], k) * SCALE + ts_ref[tb, g]
            picked = _tile_rows(_dot(sels[g], ex_ref[:, pl.ds(k0, SEL_KEY_TILE)]))
            valid = (picked > 0.5) & causal
            lg = jnp.where(valid, lg, NEG)
            m_new = jnp.maximum(m, jnp.max(lg, axis=-1, keepdims=True))
            p = jnp.where(valid, jnp.exp(lg - m_new), 0.0)
            alpha = jnp.exp(m - m_new)
            l = alpha * l + jnp.sum(p, axis=-1, keepdims=True)
            acc = alpha * acc + _dot(p.astype(BF16), v)
            out.append((m_new, l, acc))
        return tuple(out)

    init = (jnp.full((rq, 1), NEG, F32), jnp.zeros((rq, 1), F32), jnp.zeros((rq, HEAD_DIM), F32))
    fin = lax.fori_loop(0, n_tiles, body, (init,) * N_KV)

    for g in range(N_KV):
        qg, o_c = qgs[g], o_cs[g]
        gl = slice(g * HEAD_DIM, (g + 1) * HEAD_DIM)
        o_s = fin[g][2] / fin[g][1]

        w0 = pl.multiple_of(i * nq, nq)
        kw = wp_ref[0, pl.ds(w0, nwk), gl]
        vw = wp_ref[0, pl.ds(w0, nwk), KV_DIM + g * HEAD_DIM:KV_DIM + (g + 1) * HEAD_DIM]
        lw = _dot_nt(qg, kw) * SCALE + tw_ref[g]
        jw = lax.broadcasted_iota(I32, lw.shape, 1)
        dist = WINDOW + qrow - jw
        valid_w = (i * nq - WINDOW + jw >= 0) & (dist >= 0) & (dist < WINDOW)
        o_w = _dot(_masked_softmax(lw, valid_w).astype(BF16), vw)
        _gate_out(o_c, o_s, o_w, gates, g, nq, out_ref)


def _attn_prompt(q, gates, kc, vc, kvb, rel_bias):
    b, l, _ = q.shape
    nq = Q_BLOCK
    nblk = l // nq
    ncl = kc.shape[1]
    ncmp = ncl - 1
    ns = max(-(-l // SEL_BLOCK), TOP_N)
    nwk = WINDOW + nq + (LANES - nq)
    qi = np.arange(nq)
    blk = np.arange(nblk)
    jc = np.arange(ncl)
    idx_c = _bucket_np(blk[:, None, None] * nq + qi[None, :, None] - (jc[None, None, :] * CMP_STRIDE + CMP_BLOCK - 1))
    n_ts = min(nblk, (MAX_DIST + SEL_KEY_TILE) // nq + 1)
    mt = np.arange(n_ts)
    kj = np.arange(SEL_KEY_TILE)
    idx_s = _bucket_np(mt[:, None, None] * nq + qi[None, :, None] - kj[None, None, :])
    jw = np.arange(nwk)
    idx_w = _bucket_np(WINDOW + qi[:, None] - jw[None, :])
    tc = _expand_bias(idx_c, rel_bias)
    ts = _expand_bias(idx_s, rel_bias)
    tw = _expand_bias(idx_w, rel_bias)
    inter = jnp.asarray(_inter_np(ncmp, ns)[:ncl], BF16)
    lk = -(-l // SEL_KEY_TILE) * SEL_KEY_TILE
    ex = jnp.asarray(_expand_np(lk), BF16)
    kvs = kvb[:, :, 0:2 * KV_DIM]
    if lk != l:
        kvs = jnp.pad(kvs, ((0, 0), (0, lk - l), (0, 0)))
    wp = jnp.pad(kvb[:, :, 2 * KV_DIM:4 * KV_DIM], ((0, 0), (WINDOW, nwk - WINDOW - nq), (0, 0)))
    lw = wp.shape[1]
    kern = functools.partial(_attn_prompt_kernel, ncmp=ncmp, ns=ns, nwk=nwk)
    return pl.pallas_call(
        kern, grid=(b, nblk), name="attn_prompt",
        in_specs=[pl.BlockSpec((1, nq, ATTN_DIM), lambda bi, i: (bi, i, 0)),
                  pl.BlockSpec((1, nq, LANES), lambda bi, i: (bi, i, 0)),
                  pl.BlockSpec((1, ncl, KV_DIM), lambda bi, i: (bi, 0, 0)),
                  pl.BlockSpec((1, ncl, KV_DIM), lambda bi, i: (bi, 0, 0)),
                  pl.BlockSpec((1, lk, 2 * KV_DIM), lambda bi, i: (bi, 0, 0)),
                  pl.BlockSpec((1, lw, 2 * KV_DIM), lambda bi, i: (bi, 0, 0)),
                  pl.BlockSpec((1, N_KV, HPG * nq, ncl), lambda bi, i: (i, 0, 0, 0)),
                  _const_spec(ts.shape), _const_spec(tw.shape), _const_spec(inter.shape), _const_spec(ex.shape)],
        out_specs=pl.BlockSpec((1, nq, ATTN_DIM), lambda bi, i: (bi, i, 0)),
        out_shape=jax.ShapeDtypeStruct((b, l, ATTN_DIM), F32),
        compiler_params=_cparams("arbitrary", "arbitrary"),
    )(q, gates, kc, vc, kvs, wp, tc, ts, tw, inter, ex)


def _attn_sample_kernel(pt_ref, *refs, n_pages, ns, nq, win_len):
    pages = refs[:n_pages]
    (rows_ref, wrows_ref, state_ref, q_ref, gt_ref, w1r_ref, pe_ref, w2_ref, kg_ref,
     tc_ref, ts_ref, tn_ref, tw_ref, inter_ref, ex_ref, perm_ref, out_ref, win_ref) = refs[n_pages:]
    del pt_ref
    page = pages[0].shape[1] // KV_ROWS
    cpp = page // CMP_STRIDE

    def page_rows(p, slot, g):
        return pages[p][0, pl.ds(slot * N_KV + g, page, stride=KV_ROWS), :].astype(BF16)
    past = n_pages * page
    ncl = past // CMP_STRIDE
    rq = HPG * nq
    q = q_ref[0]
    gates = gt_ref[0]
    qrow = lax.broadcasted_iota(I32, (rq, 1), 0) % nq
    cur = past // SEL_BLOCK
    pad_new = jnp.zeros((LANES - nq, HEAD_DIM), BF16)
    lane_new = lax.broadcasted_iota(I32, (rq, LANES), 1)
    valid_new = lane_new <= qrow

    win_ref[0, 0:win_len - nq, :] = state_ref[0, nq:win_len, :]
    win_ref[0, win_len - nq:win_len, :] = wrows_ref[0]

    for g in range(N_KV):
        qg = _group_q(q, g)
        gl = slice(g * HEAD_DIM, (g + 1) * HEAD_DIM)

        def cmp_branch(br):
            by_s = [_dot(perm_ref[...], page_rows(p, br, g)).astype(BF16) for p in range(n_pages)]
            slabs = [jnp.concatenate([r[s * cpp:(s + 1) * cpp] for r in by_s], axis=0) for s in range(CMP_STRIDE)]
            return _compress_one(slabs, w1r_ref[br], pe_ref[br, 0:1, :], w2_ref[br])

        kc = _head_norm(cmp_branch(0), kg_ref[...]).astype(BF16)
        vc = cmp_branch(1).astype(BF16)
        lc = _dot_nt(qg, kc) * SCALE + tc_ref[g]
        jc = lax.broadcasted_iota(I32, lc.shape, 1)
        pr_c = _masked_softmax(lc, jc < ncl - 1)
        o_c = _dot(pr_c.astype(BF16), vc)
        sel = _select_blocks(pr_c, inter_ref[...], cur, nq, ns)

        c_k = 2 * KV_DIM + g * HEAD_DIM
        c_v = 3 * KV_DIM + g * HEAD_DIM
        picked = _tile_rows(_dot(sel, ex_ref[...]))
        lg = jnp.concatenate([_dot_nt(qg, page_rows(p, 2, g)) for p in range(n_pages)], axis=1) * SCALE + ts_ref[g]
        valid = picked > 0.5
        lg = jnp.where(valid, lg, NEG)
        k_new = jnp.concatenate([rows_ref[0, :, c_k:c_k + HEAD_DIM].astype(BF16), pad_new], axis=0)
        v_new = jnp.concatenate([rows_ref[0, :, c_v:c_v + HEAD_DIM].astype(BF16), pad_new], axis=0)
        ln = jnp.where(valid_new, _dot_nt(qg, k_new) * SCALE + tn_ref[g], NEG)
        m = jnp.maximum(jnp.max(lg, axis=-1, keepdims=True), jnp.max(ln, axis=-1, keepdims=True))
        p_old = jnp.where(valid, jnp.exp(lg - m), 0.0)
        p_new = jnp.where(valid_new, jnp.exp(ln - m), 0.0)
        den = jnp.sum(p_old, axis=-1, keepdims=True) + jnp.sum(p_new, axis=-1, keepdims=True)
        p_old = (p_old / den).astype(BF16)
        o_s = _dot((p_new / den).astype(BF16), v_new)
        for p in range(n_pages):
            o_s = o_s + _dot(p_old[:, p * page:(p + 1) * page], page_rows(p, 3, g))

        kw = state_ref[0, :, gl].astype(BF16)
        vw = state_ref[0, :, KV_DIM + g * HEAD_DIM:KV_DIM + (g + 1) * HEAD_DIM].astype(BF16)
        lw = _dot_nt(qg, kw) * SCALE + tw_ref[g]
        jw = lax.broadcasted_iota(I32, lw.shape, 1)
        valid_w = jw > qrow + (win_len - WINDOW)
        lw = jnp.where(valid_w, lw, NEG)
        kwn = jnp.concatenate([wrows_ref[0, :, gl].astype(BF16), pad_new], axis=0)
        vwn = jnp.concatenate([wrows_ref[0, :, KV_DIM + g * HEAD_DIM:KV_DIM + (g + 1) * HEAD_DIM].astype(BF16),
                               pad_new], axis=0)
        lwn = jnp.where(valid_new, _dot_nt(qg, kwn) * SCALE + tn_ref[g], NEG)
        m = jnp.maximum(jnp.max(lw, axis=-1, keepdims=True), jnp.max(lwn, axis=-1, keepdims=True))
        pw_old = jnp.where(valid_w, jnp.exp(lw - m), 0.0)
        pw_new = jnp.where(valid_new, jnp.exp(lwn - m), 0.0)
        den = jnp.sum(pw_old, axis=-1, keepdims=True) + jnp.sum(pw_new, axis=-1, keepdims=True)
        o_w = _dot((pw_old / den).astype(BF16), vw) + _dot((pw_new / den).astype(BF16), vwn)
        _gate_out(o_c, o_s, o_w, gates, g, nq, out_ref)


def _attn_sample(q, gates, rows, wrows, cache, state, page_table, w1r, pe_hid, w2, kc_gain, rel_bias):
    nb, nq, _ = q.shape
    n_pages = page_table.shape[1]
    page = cache.shape[1] // KV_ROWS
    past = n_pages * page
    win_len = state.shape[1]
    t_all = past + nq
    ns = max(-(-t_all // SEL_BLOCK), TOP_N)
    ncl = past // CMP_STRIDE
    qi = np.arange(nq)
    idx_c = _bucket_np(past + qi[:, None] - (np.arange(ncl)[None, :] * CMP_STRIDE + CMP_BLOCK - 1))
    idx_s = _bucket_np(past + qi[:, None] - np.arange(past)[None, :])
    idx_n = _bucket_np(qi[:, None] - np.arange(LANES)[None, :])
    idx_w = _bucket_np(win_len + qi[:, None] - np.arange(win_len)[None, :])
    tc = _expand_bias(idx_c, rel_bias)
    ts = _expand_bias(idx_s, rel_bias)
    tn = _expand_bias(idx_n, rel_bias)
    tw = _expand_bias(idx_w, rel_bias)
    inter = jnp.asarray(_inter_np(ncl - 1, ns)[:ncl], BF16)
    ex = jnp.asarray(_expand_np(past), BF16)
    tok = np.arange(page)
    perm_np = ((tok % CMP_STRIDE) * (page // CMP_STRIDE) + tok // CMP_STRIDE)[None, :] == tok[:, None]
    perm = jnp.asarray(perm_np.astype(np.float32), BF16)
    kern = functools.partial(_attn_sample_kernel, n_pages=n_pages, ns=ns, nq=nq, win_len=win_len)
    page_specs = [pl.BlockSpec((1,) + cache.shape[1:], functools.partial(lambda bi, pt, k: (pt[bi, k], 0, 0), k=k))
                  for k in range(n_pages)]
    per_b = lambda shape: pl.BlockSpec((1,) + shape, lambda bi, pt: (bi, 0, 0))
    cst = lambda a: pl.BlockSpec(a.shape, lambda bi, pt: (0,) * a.ndim, pipeline_mode=pl.Buffered(1))
    grid_spec = pltpu.PrefetchScalarGridSpec(
        num_scalar_prefetch=1, grid=(nb,),
        in_specs=page_specs + [per_b((nq, 4 * KV_DIM)), per_b((nq, 2 * KV_DIM)), per_b((win_len, 2 * KV_DIM)),
                               per_b((nq, ATTN_DIM)), per_b((nq, LANES)),
                               cst(w1r), cst(pe_hid), cst(w2), cst(kc_gain),
                               cst(tc), cst(ts), cst(tn), cst(tw), cst(inter), cst(ex), cst(perm)],
        out_specs=[per_b((nq, ATTN_DIM)), per_b((win_len, 2 * KV_DIM))])
    return pl.pallas_call(
        kern, grid_spec=grid_spec, name="attn_sample",
        out_shape=[jax.ShapeDtypeStruct((nb, nq, ATTN_DIM), F32),
                   jax.ShapeDtypeStruct((nb, win_len, 2 * KV_DIM), F32)],
        compiler_params=_cparams("arbitrary"),
    )(page_table, *([cache] * n_pages), rows, wrows, state, q, gates, w1r, pe_hid, w2, kc_gain,
      tc, ts, tn, tw, inter, ex, perm)


def _merge_kernel(attn_ref, cn_ref, x_ref, ga_ref, w_ref, h_ref):
    a = _rms(attn_ref[...], ga_ref[...]).astype(BF16)
    ac = jnp.concatenate([a, cn_ref[...]], axis=1)
    h_ref[...] = x_ref[...] + _dot(ac, w_ref[...])


def _merge(attn, cn, x2, gain_a, w_out):
    n = x2.shape[0]
    tm = _token_tile(n)
    row = lambda w: pl.BlockSpec((tm, w), lambda i: (i, 0))
    return pl.pallas_call(
        _merge_kernel, grid=(n // tm,), name="merge",
        in_specs=[row(ATTN_DIM), row(CONV_DIM), row(D_MODEL), _const_spec((1, ATTN_DIM)), _const_spec(w_out.shape)],
        out_specs=row(D_MODEL),
        out_shape=jax.ShapeDtypeStruct((n, D_MODEL), F32),
        compiler_params=_cparams("arbitrary"),
    )(attn, cn, x2, gain_a, w_out)


def _top_rows(v, k, payload=None):
    nrow = v.shape[0]
    rid = lax.broadcasted_iota(I32, v.shape, 0).astype(F32)
    vals, ids = [], []
    for _ in range(k):
        m = jnp.max(v, axis=0, keepdims=True)
        am = jnp.min(jnp.where(v == m, rid, float(nrow)), axis=0, keepdims=True)
        hit = rid == am
        vals.append(m)
        if payload is None:
            ids.append(am)
        else:
            ids.append(jnp.max(jnp.where(hit, payload, -1.0), axis=0, keepdims=True))
        v = jnp.where(hit, -jnp.inf, v)
    return jnp.concatenate(vals, axis=0), jnp.concatenate(ids, axis=0)


def _route_kernel(h_ref, g2_ref, wq_ref, keys_ref, n2_ref, eidx_ref, gw_ref, qh_ref):
    n2 = _rms(h_ref[...], g2_ref[...])
    n2_ref[...] = n2
    qh_ref[...] = _dot(n2.astype(BF16), wq_ref[...])
    half = PEER_DK // 2

    def head(r, carry):
        sv, si = [], []
        for hf in range(2):
            c0 = pl.multiple_of(r * PEER_DK + hf * half, half)
            qc = qh_ref[:, pl.ds(c0, half)].astype(BF16)
            st = _dot_nt(keys_ref[hf], qc)
            v, ix = _top_rows(st, PEER_TOPK)
            sv.append(v)
            si.append(ix)
        sub = lax.broadcasted_iota(I32, (SUBLANES, st.shape[1]), 0)
        cand, cidx = [], []
        a = 0
        while PEER_TOPK // (a + 1) > 1:
            nb = PEER_TOPK // (a + 1)
            for b0 in range(0, nb, SUBLANES):
                s = sv[0][a:a + 1] + sv[1][b0:b0 + SUBLANES]
                cand.append(s if b0 + SUBLANES <= nb else jnp.where(sub < nb - b0, s, -jnp.inf))
                cidx.append(si[0][a:a + 1] * float(N_KEYS) + si[1][b0:b0 + SUBLANES])
            a += 1
        assert PEER_TOPK - a == SUBLANES
        cand.append(sv[0][a:PEER_TOPK] + sv[1][0:1])
        cidx.append(si[0][a:PEER_TOPK] * float(N_KEYS) + si[1][0:1])
        fv, fe = _top_rows(jnp.concatenate(cand, axis=0), PEER_TOPK, payload=jnp.concatenate(cidx, axis=0))
        e = jnp.exp(fv - fv[0:1])
        gw_ref[r] = e / jnp.sum(e, axis=0, keepdims=True)
        eidx_ref[r] = fe.astype(I32)
        return carry

    lax.fori_loop(0, PEER_HEADS, head, 0)


def _route(h, g2, wq, keys):
    n = h.shape[0]
    tm = _token_tile(n)
    row = lambda w: pl.BlockSpec((tm, w), lambda i: (i, 0))
    hk = pl.BlockSpec((PEER_HEADS, PEER_TOPK, tm), lambda i: (0, 0, i))
    return pl.pallas_call(
        _route_kernel, grid=(n // tm,), name="peer_route",
        in_specs=[row(D_MODEL), _const_spec((1, D_MODEL)), _const_spec(wq.shape), _const_spec(keys.shape)],
        out_specs=[row(D_MODEL), hk, hk],
        out_shape=[jax.ShapeDtypeStruct((n, D_MODEL), F32),
                   jax.ShapeDtypeStruct((PEER_HEADS, PEER_TOPK, n), I32),
                   jax.ShapeDtypeStruct((PEER_HEADS, PEER_TOPK, n), F32)],
        scratch_shapes=[pltpu.VMEM((tm, PEER_HEADS * PEER_DK), F32)],
        compiler_params=_cparams("arbitrary"),
    )(h, g2, wq, keys)


_BITREV = (0, 4, 2, 6, 1, 5, 3, 7)


def _sublane_sums(vs):
    sub = lax.broadcasted_iota(I32, (SUBLANES, LANES), 0)
    cur = [vs[j] for j in _BITREV]
    sh = SUBLANES // 2
    while sh >= 1:
        low = (sub & sh) == 0
        nxt = []
        for j in range(0, len(cur), 2):
            a, b = cur[j], cur[j + 1]
            nxt.append(jnp.where(low, a, pltpu.roll(b, sh, 0)) + jnp.where(low, pltpu.roll(a, SUBLANES - sh, 0), b))
        cur = nxt
        sh //= 2
    return cur[0]


def _peer_kernel(idx_cur_ref, idx_nxt_ref, x_ref, h_ref, gw_ref, tab_ref, out_ref, buf_a, buf_b, sem_ref, *, nsteps):
    i = pl.program_id(0)
    tt = PEER_TOKENS
    nrow = tt * PEER_PAIRS
    half = tab_ref.shape[1] // 2
    lane = lax.broadcasted_iota(I32, (SUBLANES, gw_ref.shape[1]), 1)
    tok0 = (i * 2 * tt) % gw_ref.shape[1]

    def fetch(idx_row, j, dst_rows, sem):
        pltpu.make_async_copy(tab_ref.at[idx_row[j]], dst_rows.at[j], sem).start()

    def wait_rows(dst, sem):
        pltpu.make_async_copy(tab_ref.at[pl.ds(0, nrow)], dst, sem).wait()

    def half_step(src, tok_off, idx_ref, idx_off, dst, dst_sem):
        def token(t, carry):
            x = x_ref[tok_off + t]
            x_lo = x[0:SUBLANES]
            x_hi = x[SUBLANES:2 * SUBLANES]
            my_lane = lane == tok0 + tok_off + t
            row0 = pl.multiple_of(t * PEER_PAIRS, PEER_PAIRS)
            src_rows = src.at[pl.ds(row0, PEER_PAIRS)]
            if dst is not None:
                dst_rows = dst.at[pl.ds(row0, PEER_PAIRS)]
                idx_row = idx_ref.at[0, 0, pl.ds(idx_off + row0, PEER_PAIRS)]
            acc_lo = jnp.zeros((SUBLANES, LANES), F32)
            acc_hi = jnp.zeros((SUBLANES, LANES), F32)
            for grp in range(PEER_PAIRS // SUBLANES):
                base = grp * SUBLANES
                prods = []
                for k in range(SUBLANES):
                    if dst is not None:
                        fetch(idx_row, base + k, dst_rows, dst_sem)
                    u = src_rows[base + k, 0:half, :].astype(F32)
                    prods.append(u[0:SUBLANES] * x_lo + u[SUBLANES:2 * SUBLANES] * x_hi)
                act = jnp.sum(_sublane_sums(prods), axis=-1, keepdims=True)
                gsub = jnp.sum(jnp.where(my_lane, gw_ref[base:base + SUBLANES, :], 0.0), axis=-1, keepdims=True)
                coef = jnp.broadcast_to(_gelu(act) * gsub, (SUBLANES, LANES))
                for k in range(SUBLANES):
                    v = src_rows[base + k, half:2 * half, :].astype(F32)
                    ck = jnp.broadcast_to(coef[k:k + 1, :], (SUBLANES, LANES))
                    acc_lo = acc_lo + ck * v[0:SUBLANES]
                    acc_hi = acc_hi + ck * v[SUBLANES:2 * SUBLANES]
            out_ref[tok_off + t] = h_ref[tok_off + t] + jnp.concatenate([acc_lo, acc_hi], axis=0)
            return carry

        lax.fori_loop(0, tt, token, 0)

    @pl.when(i == 0)
    def _():
        def prime(j, carry):
            fetch(idx_cur_ref.at[0, 0], j, buf_a, sem_ref.at[0])
            return carry
        lax.fori_loop(0, nrow, prime, 0)

    wait_rows(buf_a, sem_ref.at[0])
    half_step(buf_a, 0, idx_cur_ref, nrow, buf_b, sem_ref.at[1])
    wait_rows(buf_b, sem_ref.at[1])

    @pl.when(i < nsteps - 1)
    def _():
        half_step(buf_b, tt, idx_nxt_ref, 0, buf_a, sem_ref.at[0])

    @pl.when(i == nsteps - 1)
    def _():
        half_step(buf_b, tt, None, 0, None, None)


def _peer(n2, h, eidx, gw, tab):
    n = n2.shape[0]
    tstep = 2 * PEER_TOKENS
    nsteps = n // tstep
    sub = D_MODEL // LANES
    x3 = n2.reshape(n, sub, LANES)
    h3 = h.reshape(n, sub, LANES)
    idx3 = eidx.reshape(nsteps, 1, tstep * PEER_PAIRS)
    tok = pl.BlockSpec((tstep, sub, LANES), lambda i: (i, 0, 0))
    smem = lambda f: pl.BlockSpec((1, 1, tstep * PEER_PAIRS), f, memory_space=pltpu.SMEM)
    gw_lanes = LANES if n % LANES == 0 else n
    rows = pltpu.VMEM((PEER_TOKENS * PEER_PAIRS,) + tab.shape[1:], tab.dtype)
    out = pl.pallas_call(
        functools.partial(_peer_kernel, nsteps=nsteps), grid=(nsteps,), name="peer_ffn",
        in_specs=[smem(lambda i: (i, 0, 0)), smem(lambda i: (jnp.minimum(i + 1, nsteps - 1), 0, 0)),
                  tok, tok,
                  pl.BlockSpec((PEER_PAIRS, gw_lanes), lambda i: (0, (i * tstep) // gw_lanes)),
                  pl.BlockSpec(memory_space=pl.ANY)],
        out_specs=tok,
        out_shape=jax.ShapeDtypeStruct((n, sub, LANES), F32),
        scratch_shapes=[rows, rows, pltpu.SemaphoreType.DMA((2,))],
        compiler_params=_cparams("arbitrary"),
    )(idx3, idx3, x3, h3, gw, tab)
    return out.reshape(n, D_MODEL)


def _ple_kernel(h_ref, p_ref, g3_ref, wg_ref, wp_ref, y_ref):
    h = h_ref[...]
    gate = _sigmoid(_dot(_rms(h, g3_ref[...]).astype(BF16), wg_ref[...]))
    y_ref[...] = h + gate * _dot(p_ref[...].astype(BF16), wp_ref[...])


def _ple(h, p, g3, w_gate, w_ple):
    n = h.shape[0]
    tm = _token_tile(n)
    row = lambda w: pl.BlockSpec((tm, w), lambda i: (i, 0))
    return pl.pallas_call(
        _ple_kernel, grid=(n // tm,), name="ple",
        in_specs=[row(D_MODEL), row(PLE_DIM), _const_spec((1, D_MODEL)), _const_spec(w_gate.shape),
                  _const_spec(w_ple.shape)],
        out_specs=row(D_MODEL),
        out_shape=jax.ShapeDtypeStruct((n, D_MODEL), F32),
        compiler_params=_cparams("arbitrary"),
    )(h, p, g3, w_gate, w_ple)


def _tail(attn, cn, x2, p2, wts):
    h = _merge(attn, cn, x2, wts["gain_a"], wts["w_out"])
    n2, eidx_t, gw_t = _route(h, wts["g2"], wts["wq"], wts["keys"])
    n = h.shape[0]
    eidx = eidx_t.reshape(PEER_PAIRS, n).T
    gw = gw_t.reshape(PEER_PAIRS, n)
    h = _peer(n2, h, eidx, gw, wts["tab"])
    return _ple(h, p2, wts["g3"], wts["w_gate"], wts["w_ple"])


def kernel(x_prompt, x_sample, p_prompt, p_sample, cache_kv, state_win, state_conv, page_table, rel_bias, norm1,
           w_in, qk_gain, cmp_pe, cmp_w1, cmp_w2, conv_w, conv_b, out_gain, w_out, norm2, peer_wq, peer_keys,
           peer_u, peer_v, norm3, ple_gate, ple_proj):
    depth = norm1.shape[0]
    assert depth == 1
    l0 = 0
    b, l, d = x_prompt.shape
    nb, nq, _ = x_sample.shape
    n_pool, page = cache_kv.shape[1], cache_kv.shape[2]
    win_len = state_win.shape[2]

    w = w_in[l0]
    c_gt = ATTN_DIM + 6 * KV_DIM
    c_cv = c_gt + N_BRANCH * N_HEADS
    w_qkv = w[:, 0:c_gt].astype(BF16)
    w_cv = jnp.concatenate([w[:, c_cv:], jnp.pad(w[:, c_gt:c_cv], ((0, 0), (0, LANES - N_BRANCH * N_HEADS)))],
                           axis=1).astype(BF16)
    g1 = norm1[l0].reshape(1, d)
    w1 = cmp_w1[l0]
    half_k = CMP_STRIDE * HEAD_DIM
    w1r = jnp.concatenate([w1[:, 0:half_k], w1[:, half_k:2 * half_k]], axis=2).astype(BF16)
    w2 = cmp_w2[l0].astype(BF16)
    pe_hid = _pe_hid(cmp_pe[l0], w1)
    kc_gain = qk_gain[l0, 1].reshape(1, HEAD_DIM)
    wts = dict(
        gain_a=out_gain[l0, 0:ATTN_DIM].reshape(1, ATTN_DIM), w_out=w_out[l0].astype(BF16),
        g2=norm2[l0].reshape(1, d), wq=peer_wq[l0].astype(BF16), keys=peer_keys[l0].astype(BF16),
        tab=jnp.concatenate([peer_u[l0].astype(BF16).reshape(-1, d // LANES, LANES),
                             peer_v[l0].astype(BF16).reshape(-1, d // LANES, LANES)], axis=1),
        g3=norm3[l0].reshape(1, d), w_gate=ple_gate[l0].astype(BF16), w_ple=ple_proj[l0].astype(BF16))
    gain_c = out_gain[l0, ATTN_DIM:].reshape(1, CONV_DIM)
    cw = conv_w[l0]
    cb = conv_b[l0].reshape(1, CONV_DIM)

    xp = x_prompt.reshape(b * l, d)
    q, rows, wrows, kvb, bu, gates = _mix(xp, g1, w_qkv, w_cv, qk_gain[l0])
    rows3 = rows.reshape(b, l, 4 * KV_DIM)
    kc, vc = _compress_prompt(rows3, w1r, pe_hid, w2, kc_gain)
    attn = _attn_prompt(q.reshape(b, l, ATTN_DIM), gates.reshape(b, l, LANES), kc, vc,
                        kvb.reshape(b, l, 4 * KV_DIM), rel_bias)
    cn, cst_p = _conv(bu.reshape(b, l, 2 * CONV_DIM), jnp.zeros((b, CONV_WIDTH - 1, CONV_DIM), F32), cw, cb, gain_c)
    y_p = _tail(attn.reshape(b * l, ATTN_DIM), cn.reshape(b * l, CONV_DIM), xp,
                p_prompt[l0].reshape(b * l, PLE_DIM), wts)
    wl = min(WINDOW, l)
    kv_p = rows3.reshape(1, b, l, 4, N_KV, HEAD_DIM)
    win_p = wrows.reshape(b, l, 2, N_KV, HEAD_DIM)[None, :, l - wl:]
    conv_p = cst_p[None]

    xs = x_sample.reshape(nb * nq, d)
    q, rows, wrows, kvb, bu, gates = _mix(xs, g1, w_qkv, w_cv, qk_gain[l0])
    attn, win_s = _attn_sample(q.reshape(nb, nq, ATTN_DIM), gates.reshape(nb, nq, LANES),
                               rows.reshape(nb, nq, 4 * KV_DIM), wrows.reshape(nb, nq, 2 * KV_DIM),
                               cache_kv[l0].reshape(n_pool, page * KV_ROWS, LANES),
                               state_win[l0].reshape(nb, win_len, 2 * KV_DIM), page_table,
                               w1r, pe_hid, w2, kc_gain, rel_bias)
    cn, cst_s = _conv(bu.reshape(nb, nq, 2 * CONV_DIM), state_conv[l0], cw, cb, gain_c)
    y_s = _tail(attn.reshape(nb * nq, ATTN_DIM), cn.reshape(nb * nq, CONV_DIM), xs,
                p_sample[l0].reshape(nb * nq, PLE_DIM), wts)
    kv_s = rows.reshape(1, nb, nq, 4, N_KV, HEAD_DIM)
    win_s = win_s.reshape(1, nb, win_len, 2, N_KV, HEAD_DIM)
    conv_s = cst_s[None]

    return (y_p.reshape(b, l, d), y_s.reshape(nb, nq, d), kv_p, win_p, conv_p, kv_s, win_s, conv_s)
```

```python
import functools
import math

import numpy as np
import jax
import jax.numpy as jnp
from jax import lax
from jax.experimental import pallas as pl
from jax.experimental.pallas import tpu as pltpu

F32 = jnp.float32
BF16 = jnp.bfloat16
I32 = jnp.int32

D_MODEL = 2048
HEAD_DIM = 128
N_HEADS = 8
N_KV = 2
HPG = N_HEADS // N_KV
ATTN_DIM = N_HEADS * HEAD_DIM
KV_DIM = N_KV * HEAD_DIM
CONV_DIM = D_MODEL - ATTN_DIM
CONV_WIDTH = 3
CMP_BLOCK = 32
CMP_STRIDE = 16
CMP_HID = 2 * HEAD_DIM
SEL_BLOCK = 64
TOP_N = 8
WINDOW = 512
Q_BLOCK = 64
N_BUCKETS = 32
MAX_DIST = 128
N_KEYS = 128
PEER_HEADS = 8
PEER_TOPK = 16
PEER_DK = 256
PLE_DIM = 256
N_BRANCH = 3
EPS = 1e-6
NEG = -1e30
FORCE = 1e4
SCALE = HEAD_DIM ** -0.5

LANES = 128
SUBLANES = 8
VMEM_LIMIT = 56 * 1024 * 1024
TOKEN_TILE = 256
PEER_TOKENS = 8
PEER_PAIRS = PEER_HEADS * PEER_TOPK
SEL_KEY_TILE = 256
KV_ROWS = 4 * KV_DIM // LANES
WIN_ROWS = 2 * KV_DIM // LANES


def _cparams(*sem):
    return pltpu.CompilerParams(dimension_semantics=sem, vmem_limit_bytes=VMEM_LIMIT)


def _const_spec(shape):
    nd = len(shape)
    return pl.BlockSpec(shape, lambda *_: (0,) * nd, pipeline_mode=pl.Buffered(1))


def _rms(x, g):
    ms = jnp.mean(x * x, axis=-1, keepdims=True)
    return x * lax.rsqrt(ms + EPS) * g


def _gelu(x):
    c = math.sqrt(2.0 / math.pi)
    return x * (0.5 * (1.0 + jnp.tanh(c * (x + 0.044715 * (x * x * x)))))


def _sigmoid(x):
    return 1.0 / (1.0 + jnp.exp(-x))


def _dot(a, b):
    return jnp.dot(a, b, preferred_element_type=F32)


def _dot_nt(a, b):
    return lax.dot_general(a, b, (((1,), (1,)), ((), ())), preferred_element_type=F32)


def _dot_split(a, b):
    hi = a.astype(BF16)
    r1 = a - hi.astype(F32)
    mid = r1.astype(BF16)
    lo = (r1 - mid.astype(F32)).astype(BF16)
    return _dot(hi, b) + _dot(mid, b) + _dot(lo, b)


def _bucket_np(dist):
    n = np.maximum(dist, 0)
    exact = N_BUCKETS // 2
    nf = np.maximum(n, 1).astype(np.float32)
    large = exact + (np.log(nf / np.float32(exact)) / np.float32(math.log(MAX_DIST / exact))
                     * np.float32(N_BUCKETS - exact)).astype(np.int32)
    return np.where(n < exact, n, np.minimum(large, N_BUCKETS - 1)).astype(np.int32)


def _bias_expand_kernel(tab_ref, idx_ref, out_ref):
    h = pl.program_id(0)
    idx = idx_ref[...]
    acc = jnp.zeros(idx.shape, F32)
    for b in range(N_BUCKETS):
        acc = jnp.where(idx == b, tab_ref[b, h], acc)
    out_ref[0] = acc


def _expand_bias(idx_np, rel_bias):
    lead = idx_np.shape[:-2]
    nq, nk = idx_np.shape[-2:]
    rows = int(np.prod(lead, dtype=np.int64)) * nq
    idx2 = jnp.asarray(idx_np.reshape(rows, nk))
    tr = rows
    while tr * nk * 4 > (1 << 20) and tr % 16 == 0:
        tr //= 2
    out = pl.pallas_call(
        _bias_expand_kernel, name="bias_expand",
        grid=(N_HEADS, rows // tr),
        in_specs=[pl.BlockSpec(memory_space=pltpu.SMEM),
                  pl.BlockSpec((tr, nk), lambda h, r: (r, 0))],
        out_specs=pl.BlockSpec((1, tr, nk), lambda h, r: (h, r, 0)),
        out_shape=jax.ShapeDtypeStruct((N_HEADS, rows, nk), F32),
        compiler_params=_cparams("arbitrary", "arbitrary"),
    )(rel_bias.astype(F32), idx2)
    nl = len(lead)
    out = out.reshape((N_KV, HPG) + lead + (nq, nk))
    perm = tuple(range(2, 2 + nl)) + (0, 1, 2 + nl, 3 + nl)
    out = out.transpose(perm)
    return out.reshape(lead + (N_KV, HPG * nq, nk))


def _head_norm(z, gain):
    ms = jnp.mean(z * z, axis=-1, keepdims=True)
    return z * lax.rsqrt(ms + EPS) * gain


def _mix_qkv_kernel(x_ref, g1_ref, w_ref, qg_ref, q_ref, rows_ref, wrows_ref, kvb_ref):
    n = _rms(x_ref[...], g1_ref[...]).astype(BF16)
    z = _dot(n, w_ref[...])
    hd = HEAD_DIM
    for h in range(N_HEADS):
        q_ref[:, h * hd:(h + 1) * hd] = _head_norm(z[:, h * hd:(h + 1) * hd], qg_ref[0:1, :]).astype(BF16)
    kv = ATTN_DIM
    tm = z.shape[0]

    def put_row(slot, g, val):
        rows_ref[pl.ds(slot * N_KV + g, tm, stride=KV_ROWS), :] = val

    def put_wrow(slot, g, val):
        wrows_ref[pl.ds(slot * N_KV + g, tm, stride=WIN_ROWS), :] = val

    for g in range(N_KV):
        put_row(0, g, z[:, kv + g * hd:kv + (g + 1) * hd])
        put_row(1, g, z[:, kv + KV_DIM + g * hd:kv + KV_DIM + (g + 1) * hd])
        c = kv + 2 * KV_DIM + g * hd
        ks = _head_norm(z[:, c:c + hd], qg_ref[2:3, :])
        put_row(2, g, ks)
        kvb_ref[:, g * hd:(g + 1) * hd] = ks.astype(BF16)
        c = kv + 3 * KV_DIM + g * hd
        put_row(3, g, z[:, c:c + hd])
        kvb_ref[:, KV_DIM + g * hd:KV_DIM + (g + 1) * hd] = z[:, c:c + hd].astype(BF16)
        c = kv + 4 * KV_DIM + g * hd
        kw = _head_norm(z[:, c:c + hd], qg_ref[3:4, :])
        put_wrow(0, g, kw)
        kvb_ref[:, 2 * KV_DIM + g * hd:2 * KV_DIM + (g + 1) * hd] = kw.astype(BF16)
        c = kv + 5 * KV_DIM + g * hd
        put_wrow(1, g, z[:, c:c + hd])
        kvb_ref[:, 3 * KV_DIM + g * hd:3 * KV_DIM + (g + 1) * hd] = z[:, c:c + hd].astype(BF16)


def _mix_cv_kernel(x_ref, g1_ref, w_ref, bu_ref, gates_ref):
    n = _rms(x_ref[...], g1_ref[...]).astype(BF16)
    z = _dot(n, w_ref[...])
    c = CONV_DIM
    bu_ref[:, 0:c] = z[:, 0:c]
    bu_ref[:, c:2 * c] = z[:, c:2 * c] * z[:, 2 * c:3 * c]
    gates_ref[...] = _sigmoid(z[:, 3 * c:3 * c + LANES])


def _token_tile(n):
    return TOKEN_TILE if n % TOKEN_TILE == 0 else n


def _mix(x2, g1, w_qkv, w_cv, qk_gain):
    n = x2.shape[0]
    tm = _token_tile(n)
    grid = (n // tm,)
    row = lambda w: pl.BlockSpec((tm, w), lambda i: (i, 0))
    q, rows, wrows, kvb = pl.pallas_call(
        _mix_qkv_kernel, grid=grid, name="mix_qkv",
        in_specs=[row(D_MODEL), _const_spec((1, D_MODEL)), _const_spec(w_qkv.shape), _const_spec((4, HEAD_DIM))],
        out_specs=[row(ATTN_DIM), pl.BlockSpec((tm * KV_ROWS, LANES), lambda i: (i, 0)),
                   pl.BlockSpec((tm * WIN_ROWS, LANES), lambda i: (i, 0)), row(4 * KV_DIM)],
        out_shape=[jax.ShapeDtypeStruct((n, ATTN_DIM), BF16), jax.ShapeDtypeStruct((n * KV_ROWS, LANES), F32),
                   jax.ShapeDtypeStruct((n * WIN_ROWS, LANES), F32), jax.ShapeDtypeStruct((n, 4 * KV_DIM), BF16)],
        compiler_params=_cparams("arbitrary"),
    )(x2, g1, w_qkv, qk_gain)
    bu, gates = pl.pallas_call(
        _mix_cv_kernel, grid=grid, name="mix_cv",
        in_specs=[row(D_MODEL), _const_spec((1, D_MODEL)), _const_spec(w_cv.shape)],
        out_specs=[row(2 * CONV_DIM), row(LANES)],
        out_shape=[jax.ShapeDtypeStruct((n, 2 * CONV_DIM), F32), jax.ShapeDtypeStruct((n, LANES), F32)],
        compiler_params=_cparams("arbitrary"),
    )(x2, g1, w_cv)
    return q, rows, wrows, kvb, bu, gates


def _conv_kernel(bu_ref, prev_ref, w_ref, b_ref, g_ref, cn_ref, cst_ref, carry_ref):
    @pl.when(pl.program_id(1) == 0)
    def _():
        carry_ref[...] = prev_ref[0]

    c = CONV_DIM
    bg = bu_ref[0, :, 0:c]
    u = bu_ref[0, :, c:2 * c]
    tl = u.shape[0]
    row = lax.broadcasted_iota(I32, u.shape, 0)
    c0 = carry_ref[0:1, :]
    c1 = carry_ref[1:2, :]
    u1 = jnp.where(row == 0, c1, pltpu.roll(u, 1, 0))
    u2 = jnp.where(row == 0, c0, jnp.where(row == 1, c1, pltpu.roll(u, 2, 0)))
    y = w_ref[0:1, :] * u2 + w_ref[1:2, :] * u1 + w_ref[2:3, :] * u
    conv = bg * (y + b_ref[...])
    cn_ref[0] = _rms(conv, g_ref[...]).astype(BF16)
    last = u[tl - 2:tl, :]
    carry_ref[...] = last
    cst_ref[0] = last


def _conv(bu, prev, conv_w, conv_b, gain_c):
    b, l, _ = bu.shape
    tl = 512 if l % 512 == 0 else l
    c = CONV_DIM
    return pl.pallas_call(
        _conv_kernel, grid=(b, l // tl), name="short_conv",
        in_specs=[pl.BlockSpec((1, tl, 2 * c), lambda i, t: (i, t, 0)),
                  pl.BlockSpec((1, CONV_WIDTH - 1, c), lambda i, t: (i, 0, 0)),
                  _const_spec((CONV_WIDTH, c)), _const_spec((1, c)), _const_spec((1, c))],
        out_specs=[pl.BlockSpec((1, tl, c), lambda i, t: (i, t, 0)),
                   pl.BlockSpec((1, CONV_WIDTH - 1, c), lambda i, t: (i, 0, 0))],
        out_shape=[jax.ShapeDtypeStruct((b, l, c), BF16), jax.ShapeDtypeStruct((b, CONV_WIDTH - 1, c), F32)],
        scratch_shapes=[pltpu.VMEM((CONV_WIDTH - 1, c), F32)],
        compiler_params=_cparams("arbitrary", "arbitrary"),
    )(bu, prev, conv_w, conv_b, gain_c)


def _pe_hid_kernel(pe_ref, w_ref, out_ref):
    out_ref[0] = _dot(pe_ref[0].astype(BF16), w_ref[0].astype(BF16))


def _pe_hid(cmp_pe, cmp_w1):
    k = CMP_BLOCK * HEAD_DIM
    pe = jnp.broadcast_to(cmp_pe.reshape(2, 1, k), (2, SUBLANES, k))
    return pl.pallas_call(
        _pe_hid_kernel, grid=(2,), name="pe_hid",
        in_specs=[pl.BlockSpec((1, SUBLANES, k), lambda i: (i, 0, 0)),
                  pl.BlockSpec((1, k, CMP_HID), lambda i: (i, 0, 0))],
        out_specs=pl.BlockSpec((1, SUBLANES, CMP_HID), lambda i: (i, 0, 0)),
        out_shape=jax.ShapeDtypeStruct((2, SUBLANES, CMP_HID), F32),
        compiler_params=_cparams("arbitrary"),
    )(pe, cmp_w1)


def _compress_one(slabs, w1r, pe_hid, w2):
    a = jnp.concatenate(slabs, axis=1)
    nch = a.shape[0]
    part = _dot(a, w1r)
    p0 = part[:, 0:CMP_HID]
    p1 = part[:, CMP_HID:2 * CMP_HID]
    hid = pe_hid + p0 + pltpu.roll(p1, nch - 1, 0)
    return _dot(_gelu(hid).astype(BF16), w2)


def _compress_prompt_kernel(rows_ref, w1r_ref, pe_ref, w2_ref, kg_ref, kc_ref, vc_ref):
    nch = rows_ref.shape[1] // (KV_ROWS * CMP_STRIDE)
    for br in range(2):
        for g in range(N_KV):
            j = br * N_KV + g
            slabs = [rows_ref[0, pl.ds(s * KV_ROWS + j, nch, stride=KV_ROWS * CMP_STRIDE), :].astype(BF16)
                     for s in range(CMP_STRIDE)]
            out = _compress_one(slabs, w1r_ref[br], pe_ref[br, 0:1, :], w2_ref[br])
            if br == 0:
                kc_ref[0, :, g * HEAD_DIM:(g + 1) * HEAD_DIM] = _head_norm(out, kg_ref[...]).astype(BF16)
            else:
                vc_ref[0, :, g * HEAD_DIM:(g + 1) * HEAD_DIM] = out.astype(BF16)


def _compress_prompt(rows, w1r, pe_hid, w2, kc_gain):
    b = rows.shape[0]
    t = rows.shape[1] // KV_ROWS
    nch = t // CMP_STRIDE
    out = jax.ShapeDtypeStruct((b, nch, KV_DIM), BF16)
    return pl.pallas_call(
        _compress_prompt_kernel, grid=(b,), name="compress_prompt",
        in_specs=[pl.BlockSpec((1, t * KV_ROWS, LANES), lambda i: (i, 0, 0)),
                  _const_spec(w1r.shape), _const_spec(pe_hid.shape), _const_spec(w2.shape),
                  _const_spec((1, HEAD_DIM))],
        out_specs=[pl.BlockSpec((1, nch, KV_DIM), lambda i: (i, 0, 0))] * 2,
        out_shape=[out, out],
        compiler_params=_cparams("arbitrary"),
    )(rows, w1r, pe_hid, w2, kc_gain)


def _group_q(q, g):
    return jnp.concatenate([q[:, (g * HPG + r) * HEAD_DIM:(g * HPG + r + 1) * HEAD_DIM] for r in range(HPG)], axis=0)


def _tile_rows(x):
    return jnp.concatenate([x] * HPG, axis=0)


def _masked_softmax(logits, valid):
    l = jnp.where(valid, logits, NEG)
    m = jnp.max(l, axis=-1, keepdims=True)
    p = jnp.where(valid, jnp.exp(l - m), 0.0)
    s = jnp.sum(p, axis=-1, keepdims=True)
    return p / jnp.where(s > 0.0, s, 1.0)


def _select_blocks(pr_c, inter, cur, nq, ns):
    prsum = pr_c[0:nq]
    for r in range(1, HPG):
        prsum = prsum + pr_c[r * nq:(r + 1) * nq]
    imp = _dot_split(prsum, inter)
    sid = lax.broadcasted_iota(I32, imp.shape, 1)
    imp = jnp.where((sid == 0) | (sid == cur) | (sid == cur - 1), FORCE, imp)
    imp = jnp.where(sid > cur, -jnp.inf, imp)
    cnt = jnp.zeros(imp.shape, F32)
    for j in range(ns):
        col = imp[:, j:j + 1]
        ahead = (col > imp) | ((col == imp) & (sid > j))
        cnt = cnt + jnp.where(ahead, 1.0, 0.0)
    return jnp.where(cnt < float(TOP_N), 1.0, 0.0).astype(BF16)


def _gate_out(o_c, o_s, o_w, gates, g, nq, out_ref):
    for r in range(HPG):
        h = g * HPG + r
        sl = slice(r * nq, (r + 1) * nq)
        o = (o_c[sl] * gates[:, h:h + 1] + o_s[sl] * gates[:, N_HEADS + h:N_HEADS + h + 1]
             + o_w[sl] * gates[:, 2 * N_HEADS + h:2 * N_HEADS + h + 1])
        out_ref[0, :, h * HEAD_DIM:(h + 1) * HEAD_DIM] = o


def _inter_np(ncl, ns):
    n = np.arange(LANES)[:, None]
    s = np.arange(LANES)[None, :]
    c_start = n * CMP_STRIDE
    c_end = c_start + CMP_BLOCK - 1
    s_start = s * SEL_BLOCK
    m = (c_start < s_start + SEL_BLOCK) & (c_end >= s_start) & (n < ncl) & (s < ns)
    return m.astype(np.float32)


def _expand_np(nkeys):
    s = np.arange(LANES)[:, None]
    k = np.arange(nkeys)[None, :]
    return (k // SEL_BLOCK == s).astype(np.float32)


def _attn_prompt_kernel(q_ref, gt_ref, kc_ref, vc_ref, kvs_ref, wp_ref, tc_ref, ts_ref, tw_ref,
                        inter_ref, ex_ref, out_ref, *, ncmp, ns, nwk):
    i = pl.program_id(1)
    nq = Q_BLOCK
    rq = HPG * nq
    q = q_ref[0]
    gates = gt_ref[0]
    qrow = lax.broadcasted_iota(I32, (rq, 1), 0) % nq
    qpos = i * nq + qrow
    n_tiles = (i * nq + nq - 1) // SEL_KEY_TILE + 1
    qgs, o_cs, sels = [], [], []
    for g in range(N_KV):
        qg = _group_q(q, g)
        gl = slice(g * HEAD_DIM, (g + 1) * HEAD_DIM)
        lc = _dot_nt(qg, kc_ref[0, :, gl]) * SCALE + tc_ref[0, g]
        jc = lax.broadcasted_iota(I32, lc.shape, 1)
        valid_c = (qpos - (jc * CMP_STRIDE + CMP_BLOCK - 1) >= 0) & (jc < ncmp)
        pr_c = _masked_softmax(lc, valid_c)
        qgs.append(qg)
        o_cs.append(_dot(pr_c.astype(BF16), vc_ref[0, :, gl]))
        sels.append(_select_blocks(pr_c, inter_ref[...], i, nq, ns))

    def body(kt, carry):
        k0 = pl.multiple_of(kt * SEL_KEY_TILE, SEL_KEY_TILE)
        tb = jnp.minimum(i - kt * (SEL_KEY_TILE // nq), ts_ref.shape[0] - 1)
        kpos = k0 + lax.broadcasted_iota(I32, (rq, SEL_KEY_TILE), 1)
        causal = kpos <= qpos
        out = []
        for g in range(N_KV):
            m, l, acc = carry[g]
            k = kvs_ref[0, pl.ds(k0, SEL_KEY_TILE), g * HEAD_DIM:(g + 1) * HEAD_DIM]
            v = kvs_ref[0, pl.ds(k0, SEL_KEY_TILE), KV_DIM + g * HEAD_DIM:KV_DIM + (g + 1) * HEAD_DIM]
            lg = _dot_nt(qgs[g], k) * SCALE + ts_ref[tb, g]
            picked = _tile_rows(_dot(sels[g], ex_ref[:, pl.ds(k0, SEL_KEY_TILE)]))
            valid = (picked > 0.5) & causal
            lg = jnp.where(valid, lg, NEG)
            m_new = jnp.maximum(m, jnp.max(lg, axis=-1, keepdims=True))
            p = jnp.where(valid, jnp.exp(lg - m_new), 0.0)
            alpha = jnp.exp(m - m_new)
            l = alpha * l + jnp.sum(p, axis=-1, keepdims=True)
            acc = alpha * acc + _dot(p.astype(BF16), v)
            out.append((m_new, l, acc))
        return tuple(out)

    init = (jnp.full((rq, 1), NEG, F32), jnp.zeros((rq, 1), F32), jnp.zeros((rq, HEAD_DIM), F32))
    fin = lax.fori_loop(0, n_tiles, body, (init,) * N_KV)

    for g in range(N_KV):
        qg, o_c = qgs[g], o_cs[g]
        gl = slice(g * HEAD_DIM, (g + 1) * HEAD_DIM)
        o_s = fin[g][2] / fin[g][1]

        w0 = pl.multiple_of(i * nq, nq)
        kw = wp_ref[0, pl.ds(w0, nwk), gl]
        vw = wp_ref[0, pl.ds(w0, nwk), KV_DIM + g * HEAD_DIM:KV_DIM + (g + 1) * HEAD_DIM]
        lw = _dot_nt(qg, kw) * SCALE + tw_ref[g]
        jw = lax.broadcasted_iota(I32, lw.shape, 1)
        dist = WINDOW + qrow - jw
        valid_w = (i * nq - WINDOW + jw >= 0) & (dist >= 0) & (dist < WINDOW)
        o_w = _dot(_masked_softmax(lw, valid_w).astype(BF16), vw)
        _gate_out(o_c, o_s, o_w, gates, g, nq, out_ref)


def _attn_prompt(q, gates, kc, vc, kvb, rel_bias):
    b, l, _ = q.shape
    nq = Q_BLOCK
    nblk = l // nq
    ncl = kc.shape[1]
    ncmp = ncl - 1
    ns = max(-(-l // SEL_BLOCK), TOP_N)
    nwk = WINDOW + nq + (LANES - nq)
    qi = np.arange(nq)
    blk = np.arange(nblk)
    jc = np.arange(ncl)
    idx_c = _bucket_np(blk[:, None, None] * nq + qi[None, :, None] - (jc[None, None, :] * CMP_STRIDE + CMP_BLOCK - 1))
    n_ts = min(nblk, (MAX_DIST + SEL_KEY_TILE) // nq + 1)
    mt = np.arange(n_ts)
    kj = np.arange(SEL_KEY_TILE)
    idx_s = _bucket_np(mt[:, None, None] * nq + qi[None, :, None] - kj[None, None, :])
    jw = np.arange(nwk)
    idx_w = _bucket_np(WINDOW + qi[:, None] - jw[None, :])
    tc = _expand_bias(idx_c, rel_bias)
    ts = _expand_bias(idx_s, rel_bias)
    tw = _expand_bias(idx_w, rel_bias)
    inter = jnp.asarray(_inter_np(ncmp, ns)[:ncl], BF16)
    lk = -(-l // SEL_KEY_TILE) * SEL_KEY_TILE
    ex = jnp.asarray(_expand_np(lk), BF16)
    kvs = kvb[:, :, 0:2 * KV_DIM]
    if lk != l:
        kvs = jnp.pad(kvs, ((0, 0), (0, lk - l), (0, 0)))
    wp = jnp.pad(kvb[:, :, 2 * KV_DIM:4 * KV_DIM], ((0, 0), (WINDOW, nwk - WINDOW - nq), (0, 0)))
    lw = wp.shape[1]
    kern = functools.partial(_attn_prompt_kernel, ncmp=ncmp, ns=ns, nwk=nwk)
    return pl.pallas_call(
        kern, grid=(b, nblk), name="attn_prompt",
        in_specs=[pl.BlockSpec((1, nq, ATTN_DIM), lambda bi, i: (bi, i, 0)),
                  pl.BlockSpec((1, nq, LANES), lambda bi, i: (bi, i, 0)),
                  pl.BlockSpec((1, ncl, KV_DIM), lambda bi, i: (bi, 0, 0)),
                  pl.BlockSpec((1, ncl, KV_DIM), lambda bi, i: (bi, 0, 0)),
                  pl.BlockSpec((1, lk, 2 * KV_DIM), lambda bi, i: (bi, 0, 0)),
                  pl.BlockSpec((1, lw, 2 * KV_DIM), lambda bi, i: (bi, 0, 0)),
                  pl.BlockSpec((1, N_KV, HPG * nq, ncl), lambda bi, i: (i, 0, 0, 0)),
                  _const_spec(ts.shape), _const_spec(tw.shape), _const_spec(inter.shape), _const_spec(ex.shape)],
        out_specs=pl.BlockSpec((1, nq, ATTN_DIM), lambda bi, i: (bi, i, 0)),
        out_shape=jax.ShapeDtypeStruct((b, l, ATTN_DIM), F32),
        compiler_params=_cparams("arbitrary", "arbitrary"),
    )(q, gates, kc, vc, kvs, wp, tc, ts, tw, inter, ex)


def _attn_sample_kernel(pt_ref, *refs, n_pages, ns, nq, win_len):
    pages = refs[:n_pages]
    (rows_ref, wrows_ref, state_ref, q_ref, gt_ref, w1r_ref, pe_ref, w2_ref, kg_ref,
     tc_ref, ts_ref, tn_ref, tw_ref, inter_ref, ex_ref, perm_ref, out_ref, win_ref) = refs[n_pages:]
    del pt_ref
    page = pages[0].shape[1] // KV_ROWS
    cpp = page // CMP_STRIDE

    def page_rows(p, slot, g):
        return pages[p][0, pl.ds(slot * N_KV + g, page, stride=KV_ROWS), :].astype(BF16)
    past = n_pages * page
    ncl = past // CMP_STRIDE
    rq = HPG * nq
    q = q_ref[0]
    gates = gt_ref[0]
    qrow = lax.broadcasted_iota(I32, (rq, 1), 0) % nq
    cur = past // SEL_BLOCK
    pad_new = jnp.zeros((LANES - nq, HEAD_DIM), BF16)
    lane_new = lax.broadcasted_iota(I32, (rq, LANES), 1)
    valid_new = lane_new <= qrow

    win_ref[0, 0:(win_len - nq) * WIN_ROWS, :] = state_ref[0, nq * WIN_ROWS:win_len * WIN_ROWS, :]
    win_ref[0, (win_len - nq) * WIN_ROWS:win_len * WIN_ROWS, :] = wrows_ref[0]

    def new_rows(ref, nrows, slot, g):
        return jnp.concatenate([ref[0, pl.ds(slot * N_KV + g, nq, stride=nrows), :].astype(BF16), pad_new], axis=0)

    for g in range(N_KV):
        qg = _group_q(q, g)
        gl = slice(g * HEAD_DIM, (g + 1) * HEAD_DIM)

        def cmp_branch(br):
            by_s = [_dot(perm_ref[...], page_rows(p, br, g)).astype(BF16) for p in range(n_pages)]
            slabs = [jnp.concatenate([r[s * cpp:(s + 1) * cpp] for r in by_s], axis=0) for s in range(CMP_STRIDE)]
            return _compress_one(slabs, w1r_ref[br], pe_ref[br, 0:1, :], w2_ref[br])

        kc = _head_norm(cmp_branch(0), kg_ref[...]).astype(BF16)
        vc = cmp_branch(1).astype(BF16)
        lc = _dot_nt(qg, kc) * SCALE + tc_ref[g]
        jc = lax.broadcasted_iota(I32, lc.shape, 1)
        pr_c = _masked_softmax(lc, jc < ncl - 1)
        o_c = _dot(pr_c.astype(BF16), vc)
        sel = _select_blocks(pr_c, inter_ref[...], cur, nq, ns)

        picked = _tile_rows(_dot(sel, ex_ref[...]))
        lg = jnp.concatenate([_dot_nt(qg, page_rows(p, 2, g)) for p in range(n_pages)], axis=1) * SCALE + ts_ref[g]
        valid = picked > 0.5
        lg = jnp.where(valid, lg, NEG)
        k_new = new_rows(rows_ref, KV_ROWS, 2, g)
        v_new = new_rows(rows_ref, KV_ROWS, 3, g)
        ln = jnp.where(valid_new, _dot_nt(qg, k_new) * SCALE + tn_ref[g], NEG)
        m = jnp.maximum(jnp.max(lg, axis=-1, keepdims=True), jnp.max(ln, axis=-1, keepdims=True))
        p_old = jnp.where(valid, jnp.exp(lg - m), 0.0)
        p_new = jnp.where(valid_new, jnp.exp(ln - m), 0.0)
        den = jnp.sum(p_old, axis=-1, keepdims=True) + jnp.sum(p_new, axis=-1, keepdims=True)
        p_old = (p_old / den).astype(BF16)
        o_s = _dot((p_new / den).astype(BF16), v_new)
        for p in range(n_pages):
            o_s = o_s + _dot(p_old[:, p * page:(p + 1) * page], page_rows(p, 3, g))

        kw = state_ref[0, pl.ds(g, win_len, stride=WIN_ROWS), :].astype(BF16)
        vw = state_ref[0, pl.ds(N_KV + g, win_len, stride=WIN_ROWS), :].astype(BF16)
        lw = _dot_nt(qg, kw) * SCALE + tw_ref[g]
        jw = lax.broadcasted_iota(I32, lw.shape, 1)
        valid_w = jw > qrow + (win_len - WINDOW)
        lw = jnp.where(valid_w, lw, NEG)
        kwn = new_rows(wrows_ref, WIN_ROWS, 0, g)
        vwn = new_rows(wrows_ref, WIN_ROWS, 1, g)
        lwn = jnp.where(valid_new, _dot_nt(qg, kwn) * SCALE + tn_ref[g], NEG)
        m = jnp.maximum(jnp.max(lw, axis=-1, keepdims=True), jnp.max(lwn, axis=-1, keepdims=True))
        pw_old = jnp.where(valid_w, jnp.exp(lw - m), 0.0)
        pw_new = jnp.where(valid_new, jnp.exp(lwn - m), 0.0)
        den = jnp.sum(pw_old, axis=-1, keepdims=True) + jnp.sum(pw_new, axis=-1, keepdims=True)
        o_w = _dot((pw_old / den).astype(BF16), vw) + _dot((pw_new / den).astype(BF16), vwn)
        _gate_out(o_c, o_s, o_w, gates, g, nq, out_ref)


def _attn_sample(q, gates, rows, wrows, cache, state, page_table, w1r, pe_hid, w2, kc_gain, rel_bias):
    nb, nq, _ = q.shape
    n_pages = page_table.shape[1]
    page = cache.shape[1] // KV_ROWS
    past = n_pages * page
    win_len = state.shape[1] // WIN_ROWS
    t_all = past + nq
    ns = max(-(-t_all // SEL_BLOCK), TOP_N)
    ncl = past // CMP_STRIDE
    qi = np.arange(nq)
    idx_c = _bucket_np(past + qi[:, None] - (np.arange(ncl)[None, :] * CMP_STRIDE + CMP_BLOCK - 1))
    idx_s = _bucket_np(past + qi[:, None] - np.arange(past)[None, :])
    idx_n = _bucket_np(qi[:, None] - np.arange(LANES)[None, :])
    idx_w = _bucket_np(win_len + qi[:, None] - np.arange(win_len)[None, :])
    tc = _expand_bias(idx_c, rel_bias)
    ts = _expand_bias(idx_s, rel_bias)
    tn = _expand_bias(idx_n, rel_bias)
    tw = _expand_bias(idx_w, rel_bias)
    inter = jnp.asarray(_inter_np(ncl - 1, ns)[:ncl], BF16)
    ex = jnp.asarray(_expand_np(past), BF16)
    tok = np.arange(page)
    perm_np = ((tok % CMP_STRIDE) * (page // CMP_STRIDE) + tok // CMP_STRIDE)[None, :] == tok[:, None]
    perm = jnp.asarray(perm_np.astype(np.float32), BF16)
    kern = functools.partial(_attn_sample_kernel, n_pages=n_pages, ns=ns, nq=nq, win_len=win_len)
    page_specs = [pl.BlockSpec((1,) + cache.shape[1:], functools.partial(lambda bi, pt, k: (pt[bi, k], 0, 0), k=k))
                  for k in range(n_pages)]
    per_b = lambda shape: pl.BlockSpec((1,) + shape, lambda bi, pt: (bi, 0, 0))
    cst = lambda a: pl.BlockSpec(a.shape, lambda bi, pt: (0,) * a.ndim, pipeline_mode=pl.Buffered(1))
    grid_spec = pltpu.PrefetchScalarGridSpec(
        num_scalar_prefetch=1, grid=(nb,),
        in_specs=page_specs + [per_b((nq * KV_ROWS, LANES)), per_b((nq * WIN_ROWS, LANES)),
                               per_b((win_len * WIN_ROWS, LANES)),
                               per_b((nq, ATTN_DIM)), per_b((nq, LANES)),
                               cst(w1r), cst(pe_hid), cst(w2), cst(kc_gain),
                               cst(tc), cst(ts), cst(tn), cst(tw), cst(inter), cst(ex), cst(perm)],
        out_specs=[per_b((nq, ATTN_DIM)), per_b((win_len * WIN_ROWS, LANES))])
    return pl.pallas_call(
        kern, grid_spec=grid_spec, name="attn_sample",
        out_shape=[jax.ShapeDtypeStruct((nb, nq, ATTN_DIM), F32),
                   jax.ShapeDtypeStruct((nb, win_len * WIN_ROWS, LANES), F32)],
        compiler_params=_cparams("arbitrary"),
    )(page_table, *([cache] * n_pages), rows, wrows, state, q, gates, w1r, pe_hid, w2, kc_gain,
      tc, ts, tn, tw, inter, ex, perm)


def _merge_kernel(attn_ref, cn_ref, x_ref, ga_ref, w_ref, h_ref):
    a = _rms(attn_ref[...], ga_ref[...]).astype(BF16)
    ac = jnp.concatenate([a, cn_ref[...]], axis=1)
    h_ref[...] = x_ref[...] + _dot(ac, w_ref[...])


def _merge(attn, cn, x2, gain_a, w_out):
    n = x2.shape[0]
    tm = _token_tile(n)
    row = lambda w: pl.BlockSpec((tm, w), lambda i: (i, 0))
    return pl.pallas_call(
        _merge_kernel, grid=(n // tm,), name="merge",
        in_specs=[row(ATTN_DIM), row(CONV_DIM), row(D_MODEL), _const_spec((1, ATTN_DIM)), _const_spec(w_out.shape)],
        out_specs=row(D_MODEL),
        out_shape=jax.ShapeDtypeStruct((n, D_MODEL), F32),
        compiler_params=_cparams("arbitrary"),
    )(attn, cn, x2, gain_a, w_out)


def _top_rows(v, k, payload=None):
    nrow = v.shape[0]
    rid = lax.broadcasted_iota(I32, v.shape, 0).astype(F32)
    vals, ids = [], []
    for _ in range(k):
        m = jnp.max(v, axis=0, keepdims=True)
        am = jnp.min(jnp.where(v == m, rid, float(nrow)), axis=0, keepdims=True)
        hit = rid == am
        vals.append(m)
        if payload is None:
            ids.append(am)
        else:
            ids.append(jnp.max(jnp.where(hit, payload, -1.0), axis=0, keepdims=True))
        v = jnp.where(hit, -jnp.inf, v)
    return jnp.concatenate(vals, axis=0), jnp.concatenate(ids, axis=0)


def _route_kernel(h_ref, g2_ref, wq_ref, keys_ref, n2_ref, eidx_ref, gw_ref, qh_ref):
    n2 = _rms(h_ref[...], g2_ref[...])
    n2_ref[...] = n2
    qh_ref[...] = _dot(n2.astype(BF16), wq_ref[...])
    half = PEER_DK // 2

    def head(r, carry):
        sv, si = [], []
        for hf in range(2):
            c0 = pl.multiple_of(r * PEER_DK + hf * half, half)
            qc = qh_ref[:, pl.ds(c0, half)].astype(BF16)
            st = _dot_nt(keys_ref[hf], qc)
            v, ix = _top_rows(st, PEER_TOPK)
            sv.append(v)
            si.append(ix)
        sub = lax.broadcasted_iota(I32, (SUBLANES, st.shape[1]), 0)
        cand, cidx = [], []
        a = 0
        while PEER_TOPK // (a + 1) > 1:
            nb = PEER_TOPK // (a + 1)
            for b0 in range(0, nb, SUBLANES):
                s = sv[0][a:a + 1] + sv[1][b0:b0 + SUBLANES]
                cand.append(s if b0 + SUBLANES <= nb else jnp.where(sub < nb - b0, s, -jnp.inf))
                cidx.append(si[0][a:a + 1] * float(N_KEYS) + si[1][b0:b0 + SUBLANES])
            a += 1
        assert PEER_TOPK - a == SUBLANES
        cand.append(sv[0][a:PEER_TOPK] + sv[1][0:1])
        cidx.append(si[0][a:PEER_TOPK] * float(N_KEYS) + si[1][0:1])
        fv, fe = _top_rows(jnp.concatenate(cand, axis=0), PEER_TOPK, payload=jnp.concatenate(cidx, axis=0))
        e = jnp.exp(fv - fv[0:1])
        gw_ref[r] = e / jnp.sum(e, axis=0, keepdims=True)
        eidx_ref[r] = fe.astype(I32)
        return carry

    lax.fori_loop(0, PEER_HEADS, head, 0)


def _route(h, g2, wq, keys):
    n = h.shape[0]
    tm = _token_tile(n)
    row = lambda w: pl.BlockSpec((tm, w), lambda i: (i, 0))
    hk = pl.BlockSpec((PEER_HEADS, PEER_TOPK, tm), lambda i: (0, 0, i))
    return pl.pallas_call(
        _route_kernel, grid=(n // tm,), name="peer_route",
        in_specs=[row(D_MODEL), _const_spec((1, D_MODEL)), _const_spec(wq.shape), _const_spec(keys.shape)],
        out_specs=[row(D_MODEL), hk, hk],
        out_shape=[jax.ShapeDtypeStruct((n, D_MODEL), F32),
                   jax.ShapeDtypeStruct((PEER_HEADS, PEER_TOPK, n), I32),
                   jax.ShapeDtypeStruct((PEER_HEADS, PEER_TOPK, n), F32)],
        scratch_shapes=[pltpu.VMEM((tm, PEER_HEADS * PEER_DK), F32)],
        compiler_params=_cparams("arbitrary"),
    )(h, g2, wq, keys)


_BITREV = (0, 4, 2, 6, 1, 5, 3, 7)


def _sublane_sums(vs):
    sub = lax.broadcasted_iota(I32, (SUBLANES, LANES), 0)
    cur = [vs[j] for j in _BITREV]
    sh = SUBLANES // 2
    while sh >= 1:
        low = (sub & sh) == 0
        nxt = []
        for j in range(0, len(cur), 2):
            a, b = cur[j], cur[j + 1]
            nxt.append(jnp.where(low, a, pltpu.roll(b, sh, 0)) + jnp.where(low, pltpu.roll(a, SUBLANES - sh, 0), b))
        cur = nxt
        sh //= 2
    return cur[0]


def _peer_kernel(idx_cur_ref, idx_nxt_ref, x_ref, h_ref, gw_ref, tab_ref, out_ref, buf_a, buf_b, sem_ref, *, nsteps):
    i = pl.program_id(0)
    tt = PEER_TOKENS
    nrow = tt * PEER_PAIRS
    half = tab_ref.shape[1] // 2
    lane = lax.broadcasted_iota(I32, (SUBLANES, gw_ref.shape[1]), 1)
    tok0 = (i * 2 * tt) % gw_ref.shape[1]

    def fetch(idx_row, j, dst_rows, sem, queue):
        pltpu.make_async_copy(tab_ref.at[idx_row[j]], dst_rows.at[j], sem).start(priority=queue)

    def wait_rows(dst, sem):
        pltpu.make_async_copy(tab_ref.at[pl.ds(0, nrow)], dst, sem).wait()

    def half_step(src, tok_off, idx_ref, idx_off, dst, dst_sem):
        def token(t, carry):
            x = x_ref[tok_off + t]
            x_lo = x[0:SUBLANES]
            x_hi = x[SUBLANES:2 * SUBLANES]
            my_lane = lane == tok0 + tok_off + t
            row0 = pl.multiple_of(t * PEER_PAIRS, PEER_PAIRS)
            src_rows = src.at[pl.ds(row0, PEER_PAIRS)]
            if dst is not None:
                dst_rows = dst.at[pl.ds(row0, PEER_PAIRS)]
                idx_row = idx_ref.at[0, 0, pl.ds(idx_off + row0, PEER_PAIRS)]
            acc_lo = jnp.zeros((SUBLANES, LANES), F32)
            acc_hi = jnp.zeros((SUBLANES, LANES), F32)
            for grp in range(PEER_PAIRS // SUBLANES):
                base = grp * SUBLANES
                prods = []
                for k in range(SUBLANES):
                    if dst is not None:
                        fetch(idx_row, base + k, dst_rows, dst_sem, k % 2)
                    u = src_rows[base + k, 0:half, :].astype(F32)
                    prods.append(u[0:SUBLANES] * x_lo + u[SUBLANES:2 * SUBLANES] * x_hi)
                act = jnp.sum(_sublane_sums(prods), axis=-1, keepdims=True)
                gsub = jnp.sum(jnp.where(my_lane, gw_ref[base:base + SUBLANES, :], 0.0), axis=-1, keepdims=True)
                coef = jnp.broadcast_to(_gelu(act) * gsub, (SUBLANES, LANES))
                for k in range(SUBLANES):
                    v = src_rows[base + k, half:2 * half, :].astype(F32)
                    ck = jnp.broadcast_to(coef[k:k + 1, :], (SUBLANES, LANES))
                    acc_lo = acc_lo + ck * v[0:SUBLANES]
                    acc_hi = acc_hi + ck * v[SUBLANES:2 * SUBLANES]
            out_ref[tok_off + t] = h_ref[tok_off + t] + jnp.concatenate([acc_lo, acc_hi], axis=0)
            return carry

        lax.fori_loop(0, tt, token, 0)

    @pl.when(i == 0)
    def _():
        def prime(j, carry):
            fetch(idx_cur_ref.at[0, 0], 2 * j, buf_a, sem_ref.at[0], 0)
            fetch(idx_cur_ref.at[0, 0], 2 * j + 1, buf_a, sem_ref.at[0], 1)
            return carry
        lax.fori_loop(0, nrow // 2, prime, 0)

    wait_rows(buf_a, sem_ref.at[0])
    half_step(buf_a, 0, idx_cur_ref, nrow, buf_b, sem_ref.at[1])
    wait_rows(buf_b, sem_ref.at[1])

    @pl.when(i < nsteps - 1)
    def _():
        half_step(buf_b, tt, idx_nxt_ref, 0, buf_a, sem_ref.at[0])

    @pl.when(i == nsteps - 1)
    def _():
        half_step(buf_b, tt, None, 0, None, None)


def _peer(n2, h, eidx, gw, tab):
    n = n2.shape[0]
    tstep = 2 * PEER_TOKENS
    nsteps = n // tstep
    sub = D_MODEL // LANES
    x3 = n2.reshape(n, sub, LANES)
    h3 = h.reshape(n, sub, LANES)
    idx3 = eidx.reshape(nsteps, 1, tstep * PEER_PAIRS)
    tok = pl.BlockSpec((tstep, sub, LANES), lambda i: (i, 0, 0))
    smem = lambda f: pl.BlockSpec((1, 1, tstep * PEER_PAIRS), f, memory_space=pltpu.SMEM)
    gw_lanes = LANES if n % LANES == 0 else n
    rows = pltpu.VMEM((PEER_TOKENS * PEER_PAIRS,) + tab.shape[1:], tab.dtype)
    out = pl.pallas_call(
        functools.partial(_peer_kernel, nsteps=nsteps), grid=(nsteps,), name="peer_ffn",
        in_specs=[smem(lambda i: (i, 0, 0)), smem(lambda i: (jnp.minimum(i + 1, nsteps - 1), 0, 0)),
                  tok, tok,
                  pl.BlockSpec((PEER_PAIRS, gw_lanes), lambda i: (0, (i * tstep) // gw_lanes)),
                  pl.BlockSpec(memory_space=pl.ANY)],
        out_specs=tok,
        out_shape=jax.ShapeDtypeStruct((n, sub, LANES), F32),
        scratch_shapes=[rows, rows, pltpu.SemaphoreType.DMA((2,))],
        compiler_params=_cparams("arbitrary"),
    )(idx3, idx3, x3, h3, gw, tab)
    return out.reshape(n, D_MODEL)


def _ple_kernel(h_ref, p_ref, g3_ref, wg_ref, wp_ref, y_ref):
    h = h_ref[...]
    gate = _sigmoid(_dot(_rms(h, g3_ref[...]).astype(BF16), wg_ref[...]))
    y_ref[...] = h + gate * _dot(p_ref[...].astype(BF16), wp_ref[...])


def _ple(h, p, g3, w_gate, w_ple):
    n = h.shape[0]
    tm = _token_tile(n)
    row = lambda w: pl.BlockSpec((tm, w), lambda i: (i, 0))
    return pl.pallas_call(
        _ple_kernel, grid=(n // tm,), name="ple",
        in_specs=[row(D_MODEL), row(PLE_DIM), _const_spec((1, D_MODEL)), _const_spec(w_gate.shape),
                  _const_spec(w_ple.shape)],
        out_specs=row(D_MODEL),
        out_shape=jax.ShapeDtypeStruct((n, D_MODEL), F32),
        compiler_params=_cparams("arbitrary"),
    )(h, p, g3, w_gate, w_ple)


def _tail(attn, cn, x2, p2, wts):
    h = _merge(attn, cn, x2, wts["gain_a"], wts["w_out"])
    n2, eidx_t, gw_t = _route(h, wts["g2"], wts["wq"], wts["keys"])
    n = h.shape[0]
    eidx = eidx_t.reshape(PEER_PAIRS, n).T
    gw = gw_t.reshape(PEER_PAIRS, n)
    h = _peer(n2, h, eidx, gw, wts["tab"])
    return _ple(h, p2, wts["g3"], wts["w_gate"], wts["w_ple"])


def kernel(x_prompt, x_sample, p_prompt, p_sample, cache_kv, state_win, state_conv, page_table, rel_bias, norm1,
           w_in, qk_gain, cmp_pe, cmp_w1, cmp_w2, conv_w, conv_b, out_gain, w_out, norm2, peer_wq, peer_keys,
           peer_u, peer_v, norm3, ple_gate, ple_proj):
    depth = norm1.shape[0]
    assert depth == 1
    l0 = 0
    b, l, d = x_prompt.shape
    nb, nq, _ = x_sample.shape
    n_pool, page = cache_kv.shape[1], cache_kv.shape[2]
    win_len = state_win.shape[2]

    w = w_in[l0]
    c_gt = ATTN_DIM + 6 * KV_DIM
    c_cv = c_gt + N_BRANCH * N_HEADS
    w_qkv = w[:, 0:c_gt].astype(BF16)
    w_cv = jnp.concatenate([w[:, c_cv:], jnp.pad(w[:, c_gt:c_cv], ((0, 0), (0, LANES - N_BRANCH * N_HEADS)))],
                           axis=1).astype(BF16)
    g1 = norm1[l0].reshape(1, d)
    w1 = cmp_w1[l0]
    half_k = CMP_STRIDE * HEAD_DIM
    w1r = jnp.concatenate([w1[:, 0:half_k], w1[:, half_k:2 * half_k]], axis=2).astype(BF16)
    w2 = cmp_w2[l0].astype(BF16)
    pe_hid = _pe_hid(cmp_pe[l0], w1)
    kc_gain = qk_gain[l0, 1].reshape(1, HEAD_DIM)
    wts = dict(
        gain_a=out_gain[l0, 0:ATTN_DIM].reshape(1, ATTN_DIM), w_out=w_out[l0].astype(BF16),
        g2=norm2[l0].reshape(1, d), wq=peer_wq[l0].astype(BF16), keys=peer_keys[l0].astype(BF16),
        tab=jnp.concatenate([peer_u[l0].astype(BF16).reshape(-1, d // LANES, LANES),
                             peer_v[l0].astype(BF16).reshape(-1, d // LANES, LANES)], axis=1),
        g3=norm3[l0].reshape(1, d), w_gate=ple_gate[l0].astype(BF16), w_ple=ple_proj[l0].astype(BF16))
    gain_c = out_gain[l0, ATTN_DIM:].reshape(1, CONV_DIM)
    cw = conv_w[l0]
    cb = conv_b[l0].reshape(1, CONV_DIM)

    xp = x_prompt.reshape(b * l, d)
    q, rows, wrows, kvb, bu, gates = _mix(xp, g1, w_qkv, w_cv, qk_gain[l0])
    kc, vc = _compress_prompt(rows.reshape(b, l * KV_ROWS, LANES), w1r, pe_hid, w2, kc_gain)
    attn = _attn_prompt(q.reshape(b, l, ATTN_DIM), gates.reshape(b, l, LANES), kc, vc,
                        kvb.reshape(b, l, 4 * KV_DIM), rel_bias)
    cn, cst_p = _conv(bu.reshape(b, l, 2 * CONV_DIM), jnp.zeros((b, CONV_WIDTH - 1, CONV_DIM), F32), cw, cb, gain_c)
    y_p = _tail(attn.reshape(b * l, ATTN_DIM), cn.reshape(b * l, CONV_DIM), xp,
                p_prompt[l0].reshape(b * l, PLE_DIM), wts)
    wl = min(WINDOW, l)
    kv_p = rows.reshape(1, b, l, 4, N_KV, HEAD_DIM)
    win_p = wrows.reshape(b, l, 2, N_KV, HEAD_DIM)[None, :, l - wl:]
    conv_p = cst_p[None]

    xs = x_sample.reshape(nb * nq, d)
    q, rows, wrows, kvb, bu, gates = _mix(xs, g1, w_qkv, w_cv, qk_gain[l0])
    attn, win_s = _attn_sample(q.reshape(nb, nq, ATTN_DIM), gates.reshape(nb, nq, LANES),
                               rows.reshape(nb, nq * KV_ROWS, LANES), wrows.reshape(nb, nq * WIN_ROWS, LANES),
                               cache_kv[l0].reshape(n_pool, page * KV_ROWS, LANES),
                               state_win[l0].reshape(nb, win_len * WIN_ROWS, LANES), page_table,
                               w1r, pe_hid, w2, kc_gain, rel_bias)
    cn, cst_s = _conv(bu.reshape(nb, nq, 2 * CONV_DIM), state_conv[l0], cw, cb, gain_c)
    y_s = _tail(attn.reshape(nb * nq, ATTN_DIM), cn.reshape(nb * nq, CONV_DIM), xs,
                p_sample[l0].reshape(nb * nq, PLE_DIM), wts)
    kv_s = rows.reshape(1, nb, nq, 4, N_KV, HEAD_DIM)
    win_s = win_s.reshape(1, nb, win_len, 2, N_KV, HEAD_DIM)
    conv_s = cst_s[None]

    return (y_p.reshape(b, l, d), y_s.reshape(nb, nq, d), kv_p, win_p, conv_p, kv_s, win_s, conv_s)
```

```python
import functools
import math

import numpy as np
import jax
import jax.numpy as jnp
from jax import lax
from jax.experimental import pallas as pl
from jax.experimental.pallas import tpu as pltpu

F32 = jnp.float32
BF16 = jnp.bfloat16
I32 = jnp.int32

D_MODEL = 2048
HEAD_DIM = 128
N_HEADS = 8
N_KV = 2
HPG = N_HEADS // N_KV
ATTN_DIM = N_HEADS * HEAD_DIM
KV_DIM = N_KV * HEAD_DIM
CONV_DIM = D_MODEL - ATTN_DIM
CONV_WIDTH = 3
CMP_BLOCK = 32
CMP_STRIDE = 16
CMP_HID = 2 * HEAD_DIM
SEL_BLOCK = 64
TOP_N = 8
WINDOW = 512
Q_BLOCK = 64
N_BUCKETS = 32
MAX_DIST = 128
N_KEYS = 128
PEER_HEADS = 8
PEER_TOPK = 16
PEER_DK = 256
PLE_DIM = 256
N_BRANCH = 3
EPS = 1e-6
NEG = -1e30
FORCE = 1e4
SCALE = HEAD_DIM ** -0.5

LANES = 128
SUBLANES = 8
VMEM_LIMIT = 56 * 1024 * 1024
TOKEN_TILE = 256
PEER_TOKENS = 8
PEER_PAIRS = PEER_HEADS * PEER_TOPK
PEER_PHASES = 4
SEL_KEY_TILE = 256
KV_ROWS = 4 * KV_DIM // LANES
WIN_ROWS = 2 * KV_DIM // LANES


def _cparams(*sem):
    return pltpu.CompilerParams(dimension_semantics=sem, vmem_limit_bytes=VMEM_LIMIT)


def _const_spec(shape):
    nd = len(shape)
    return pl.BlockSpec(shape, lambda *_: (0,) * nd, pipeline_mode=pl.Buffered(1))


def _rms(x, g):
    ms = jnp.mean(x * x, axis=-1, keepdims=True)
    return x * lax.rsqrt(ms + EPS) * g


def _gelu(x):
    c = math.sqrt(2.0 / math.pi)
    return x * (0.5 * (1.0 + jnp.tanh(c * (x + 0.044715 * (x * x * x)))))


def _sigmoid(x):
    return 1.0 / (1.0 + jnp.exp(-x))


def _dot(a, b):
    return jnp.dot(a, b, preferred_element_type=F32)


def _dot_nt(a, b):
    return lax.dot_general(a, b, (((1,), (1,)), ((), ())), preferred_element_type=F32)


def _dot_split(a, b):
    hi = a.astype(BF16)
    r1 = a - hi.astype(F32)
    mid = r1.astype(BF16)
    lo = (r1 - mid.astype(F32)).astype(BF16)
    return _dot(hi, b) + _dot(mid, b) + _dot(lo, b)


def _bucket_np(dist):
    n = np.maximum(dist, 0)
    exact = N_BUCKETS // 2
    nf = np.maximum(n, 1).astype(np.float32)
    large = exact + (np.log(nf / np.float32(exact)) / np.float32(math.log(MAX_DIST / exact))
                     * np.float32(N_BUCKETS - exact)).astype(np.int32)
    return np.where(n < exact, n, np.minimum(large, N_BUCKETS - 1)).astype(np.int32)


def _bias_expand_kernel(tab_ref, idx_ref, out_ref):
    h = pl.program_id(0)
    idx = idx_ref[...]
    acc = jnp.zeros(idx.shape, F32)
    for b in range(N_BUCKETS):
        acc = jnp.where(idx == b, tab_ref[b, h], acc)
    out_ref[0] = acc


def _expand_bias(idx_np, rel_bias):
    lead = idx_np.shape[:-2]
    nq, nk = idx_np.shape[-2:]
    rows = int(np.prod(lead, dtype=np.int64)) * nq
    idx2 = jnp.asarray(idx_np.reshape(rows, nk))
    tr = rows
    while tr * nk * 4 > (1 << 20) and tr % 16 == 0:
        tr //= 2
    out = pl.pallas_call(
        _bias_expand_kernel, name="bias_expand",
        grid=(N_HEADS, rows // tr),
        in_specs=[pl.BlockSpec(memory_space=pltpu.SMEM),
                  pl.BlockSpec((tr, nk), lambda h, r: (r, 0))],
        out_specs=pl.BlockSpec((1, tr, nk), lambda h, r: (h, r, 0)),
        out_shape=jax.ShapeDtypeStruct((N_HEADS, rows, nk), F32),
        compiler_params=_cparams("arbitrary", "arbitrary"),
    )(rel_bias.astype(F32), idx2)
    nl = len(lead)
    out = out.reshape((N_KV, HPG) + lead + (nq, nk))
    perm = tuple(range(2, 2 + nl)) + (0, 1, 2 + nl, 3 + nl)
    out = out.transpose(perm)
    return out.reshape(lead + (N_KV, HPG * nq, nk))


def _head_norm(z, gain):
    ms = jnp.mean(z * z, axis=-1, keepdims=True)
    return z * lax.rsqrt(ms + EPS) * gain


def _mix_qkv_kernel(x_ref, g1_ref, w_ref, qg_ref, q_ref, rows_ref, wrows_ref, kvb_ref):
    n = _rms(x_ref[...], g1_ref[...]).astype(BF16)
    z = _dot(n, w_ref[...])
    hd = HEAD_DIM
    for h in range(N_HEADS):
        q_ref[:, h * hd:(h + 1) * hd] = _head_norm(z[:, h * hd:(h + 1) * hd], qg_ref[0:1, :]).astype(BF16)
    kv = ATTN_DIM
    tm = z.shape[0]

    def put_row(slot, g, val):
        rows_ref[pl.ds(slot * N_KV + g, tm, stride=KV_ROWS), :] = val

    def put_wrow(slot, g, val):
        wrows_ref[pl.ds(slot * N_KV + g, tm, stride=WIN_ROWS), :] = val

    for g in range(N_KV):
        put_row(0, g, z[:, kv + g * hd:kv + (g + 1) * hd])
        put_row(1, g, z[:, kv + KV_DIM + g * hd:kv + KV_DIM + (g + 1) * hd])
        c = kv + 2 * KV_DIM + g * hd
        ks = _head_norm(z[:, c:c + hd], qg_ref[2:3, :])
        put_row(2, g, ks)
        kvb_ref[:, g * hd:(g + 1) * hd] = ks.astype(BF16)
        c = kv + 3 * KV_DIM + g * hd
        put_row(3, g, z[:, c:c + hd])
        kvb_ref[:, KV_DIM + g * hd:KV_DIM + (g + 1) * hd] = z[:, c:c + hd].astype(BF16)
        c = kv + 4 * KV_DIM + g * hd
        kw = _head_norm(z[:, c:c + hd], qg_ref[3:4, :])
        put_wrow(0, g, kw)
        kvb_ref[:, 2 * KV_DIM + g * hd:2 * KV_DIM + (g + 1) * hd] = kw.astype(BF16)
        c = kv + 5 * KV_DIM + g * hd
        put_wrow(1, g, z[:, c:c + hd])
        kvb_ref[:, 3 * KV_DIM + g * hd:3 * KV_DIM + (g + 1) * hd] = z[:, c:c + hd].astype(BF16)


def _mix_cv_kernel(x_ref, g1_ref, w_ref, bu_ref, gates_ref):
    n = _rms(x_ref[...], g1_ref[...]).astype(BF16)
    z = _dot(n, w_ref[...])
    c = CONV_DIM
    bu_ref[:, 0:c] = z[:, 0:c]
    bu_ref[:, c:2 * c] = z[:, c:2 * c] * z[:, 2 * c:3 * c]
    gates_ref[...] = _sigmoid(z[:, 3 * c:3 * c + LANES])


def _token_tile(n):
    return TOKEN_TILE if n % TOKEN_TILE == 0 else n


def _mix(x2, g1, w_qkv, w_cv, qk_gain):
    n = x2.shape[0]
    tm = _token_tile(n)
    grid = (n // tm,)
    row = lambda w: pl.BlockSpec((tm, w), lambda i: (i, 0))
    q, rows, wrows, kvb = pl.pallas_call(
        _mix_qkv_kernel, grid=grid, name="mix_qkv",
        in_specs=[row(D_MODEL), _const_spec((1, D_MODEL)), _const_spec(w_qkv.shape), _const_spec((4, HEAD_DIM))],
        out_specs=[row(ATTN_DIM), pl.BlockSpec((tm * KV_ROWS, LANES), lambda i: (i, 0)),
                   pl.BlockSpec((tm * WIN_ROWS, LANES), lambda i: (i, 0)), row(4 * KV_DIM)],
        out_shape=[jax.ShapeDtypeStruct((n, ATTN_DIM), BF16), jax.ShapeDtypeStruct((n * KV_ROWS, LANES), F32),
                   jax.ShapeDtypeStruct((n * WIN_ROWS, LANES), F32), jax.ShapeDtypeStruct((n, 4 * KV_DIM), BF16)],
        compiler_params=_cparams("arbitrary"),
    )(x2, g1, w_qkv, qk_gain)
    bu, gates = pl.pallas_call(
        _mix_cv_kernel, grid=grid, name="mix_cv",
        in_specs=[row(D_MODEL), _const_spec((1, D_MODEL)), _const_spec(w_cv.shape)],
        out_specs=[row(2 * CONV_DIM), row(LANES)],
        out_shape=[jax.ShapeDtypeStruct((n, 2 * CONV_DIM), F32), jax.ShapeDtypeStruct((n, LANES), F32)],
        compiler_params=_cparams("arbitrary"),
    )(x2, g1, w_cv)
    return q, rows, wrows, kvb, bu, gates


def _conv_kernel(bu_ref, prev_ref, w_ref, b_ref, g_ref, cn_ref, cst_ref, carry_ref):
    @pl.when(pl.program_id(1) == 0)
    def _():
        carry_ref[...] = prev_ref[0]

    c = CONV_DIM
    bg = bu_ref[0, :, 0:c]
    u = bu_ref[0, :, c:2 * c]
    tl = u.shape[0]
    row = lax.broadcasted_iota(I32, u.shape, 0)
    c0 = carry_ref[0:1, :]
    c1 = carry_ref[1:2, :]
    u1 = jnp.where(row == 0, c1, pltpu.roll(u, 1, 0))
    u2 = jnp.where(row == 0, c0, jnp.where(row == 1, c1, pltpu.roll(u, 2, 0)))
    y = w_ref[0:1, :] * u2 + w_ref[1:2, :] * u1 + w_ref[2:3, :] * u
    conv = bg * (y + b_ref[...])
    cn_ref[0] = _rms(conv, g_ref[...]).astype(BF16)
    last = u[tl - 2:tl, :]
    carry_ref[...] = last
    cst_ref[0] = last


def _conv(bu, prev, conv_w, conv_b, gain_c):
    b, l, _ = bu.shape
    tl = 512 if l % 512 == 0 else l
    c = CONV_DIM
    return pl.pallas_call(
        _conv_kernel, grid=(b, l // tl), name="short_conv",
        in_specs=[pl.BlockSpec((1, tl, 2 * c), lambda i, t: (i, t, 0)),
                  pl.BlockSpec((1, CONV_WIDTH - 1, c), lambda i, t: (i, 0, 0)),
                  _const_spec((CONV_WIDTH, c)), _const_spec((1, c)), _const_spec((1, c))],
        out_specs=[pl.BlockSpec((1, tl, c), lambda i, t: (i, t, 0)),
                   pl.BlockSpec((1, CONV_WIDTH - 1, c), lambda i, t: (i, 0, 0))],
        out_shape=[jax.ShapeDtypeStruct((b, l, c), BF16), jax.ShapeDtypeStruct((b, CONV_WIDTH - 1, c), F32)],
        scratch_shapes=[pltpu.VMEM((CONV_WIDTH - 1, c), F32)],
        compiler_params=_cparams("arbitrary", "arbitrary"),
    )(bu, prev, conv_w, conv_b, gain_c)


def _pe_hid_kernel(pe_ref, w_ref, out_ref):
    out_ref[0] = _dot(pe_ref[0].astype(BF16), w_ref[0].astype(BF16))


def _pe_hid(cmp_pe, cmp_w1):
    k = CMP_BLOCK * HEAD_DIM
    pe = jnp.broadcast_to(cmp_pe.reshape(2, 1, k), (2, SUBLANES, k))
    return pl.pallas_call(
        _pe_hid_kernel, grid=(2,), name="pe_hid",
        in_specs=[pl.BlockSpec((1, SUBLANES, k), lambda i: (i, 0, 0)),
                  pl.BlockSpec((1, k, CMP_HID), lambda i: (i, 0, 0))],
        out_specs=pl.BlockSpec((1, SUBLANES, CMP_HID), lambda i: (i, 0, 0)),
        out_shape=jax.ShapeDtypeStruct((2, SUBLANES, CMP_HID), F32),
        compiler_params=_cparams("arbitrary"),
    )(pe, cmp_w1)


def _compress_one(slabs, w1r, pe_hid, w2):
    a = jnp.concatenate(slabs, axis=1)
    nch = a.shape[0]
    part = _dot(a, w1r)
    p0 = part[:, 0:CMP_HID]
    p1 = part[:, CMP_HID:2 * CMP_HID]
    hid = pe_hid + p0 + pltpu.roll(p1, nch - 1, 0)
    return _dot(_gelu(hid).astype(BF16), w2)


def _compress_prompt_kernel(rows_ref, w1r_ref, pe_ref, w2_ref, kg_ref, kc_ref, vc_ref):
    nch = rows_ref.shape[1] // (KV_ROWS * CMP_STRIDE)
    for br in range(2):
        for g in range(N_KV):
            j = br * N_KV + g
            slabs = [rows_ref[0, pl.ds(s * KV_ROWS + j, nch, stride=KV_ROWS * CMP_STRIDE), :].astype(BF16)
                     for s in range(CMP_STRIDE)]
            out = _compress_one(slabs, w1r_ref[br], pe_ref[br, 0:1, :], w2_ref[br])
            if br == 0:
                kc_ref[0, :, g * HEAD_DIM:(g + 1) * HEAD_DIM] = _head_norm(out, kg_ref[...]).astype(BF16)
            else:
                vc_ref[0, :, g * HEAD_DIM:(g + 1) * HEAD_DIM] = out.astype(BF16)


def _compress_prompt(rows, w1r, pe_hid, w2, kc_gain):
    b = rows.shape[0]
    t = rows.shape[1] // KV_ROWS
    nch = t // CMP_STRIDE
    out = jax.ShapeDtypeStruct((b, nch, KV_DIM), BF16)
    return pl.pallas_call(
        _compress_prompt_kernel, grid=(b,), name="compress_prompt",
        in_specs=[pl.BlockSpec((1, t * KV_ROWS, LANES), lambda i: (i, 0, 0)),
                  _const_spec(w1r.shape), _const_spec(pe_hid.shape), _const_spec(w2.shape),
                  _const_spec((1, HEAD_DIM))],
        out_specs=[pl.BlockSpec((1, nch, KV_DIM), lambda i: (i, 0, 0))] * 2,
        out_shape=[out, out],
        compiler_params=_cparams("arbitrary"),
    )(rows, w1r, pe_hid, w2, kc_gain)


def _group_q(q, g):
    return jnp.concatenate([q[:, (g * HPG + r) * HEAD_DIM:(g * HPG + r + 1) * HEAD_DIM] for r in range(HPG)], axis=0)


def _tile_rows(x):
    return jnp.concatenate([x] * HPG, axis=0)


def _masked_softmax(logits, valid):
    l = jnp.where(valid, logits, NEG)
    m = jnp.max(l, axis=-1, keepdims=True)
    p = jnp.where(valid, jnp.exp(l - m), 0.0)
    s = jnp.sum(p, axis=-1, keepdims=True)
    return p / jnp.where(s > 0.0, s, 1.0)


def _select_blocks(pr_c, inter, cur, nq, ns):
    prsum = pr_c[0:nq]
    for r in range(1, HPG):
        prsum = prsum + pr_c[r * nq:(r + 1) * nq]
    imp = _dot_split(prsum, inter)
    sid = lax.broadcasted_iota(I32, imp.shape, 1)
    imp = jnp.where((sid == 0) | (sid == cur) | (sid == cur - 1), FORCE, imp)
    imp = jnp.where(sid > cur, -jnp.inf, imp)
    cnt = jnp.zeros(imp.shape, F32)
    for j in range(ns):
        col = imp[:, j:j + 1]
        ahead = (col > imp) | ((col == imp) & (sid > j))
        cnt = cnt + jnp.where(ahead, 1.0, 0.0)
    return jnp.where(cnt < float(TOP_N), 1.0, 0.0).astype(BF16)


def _gate_out(o_c, o_s, o_w, gates, g, nq, out_ref):
    for r in range(HPG):
        h = g * HPG + r
        sl = slice(r * nq, (r + 1) * nq)
        o = (o_c[sl] * gates[:, h:h + 1] + o_s[sl] * gates[:, N_HEADS + h:N_HEADS + h + 1]
             + o_w[sl] * gates[:, 2 * N_HEADS + h:2 * N_HEADS + h + 1])
        out_ref[0, :, h * HEAD_DIM:(h + 1) * HEAD_DIM] = o


def _inter_np(ncl, ns):
    n = np.arange(LANES)[:, None]
    s = np.arange(LANES)[None, :]
    c_start = n * CMP_STRIDE
    c_end = c_start + CMP_BLOCK - 1
    s_start = s * SEL_BLOCK
    m = (c_start < s_start + SEL_BLOCK) & (c_end >= s_start) & (n < ncl) & (s < ns)
    return m.astype(np.float32)


def _expand_np(nkeys):
    s = np.arange(LANES)[:, None]
    k = np.arange(nkeys)[None, :]
    return (k // SEL_BLOCK == s).astype(np.float32)


def _attn_prompt_kernel(q_ref, gt_ref, kc_ref, vc_ref, kvs_ref, wp_ref, tc_ref, ts_ref, tw_ref,
                        inter_ref, ex_ref, out_ref, *, ncmp, ns, nwk):
    i = pl.program_id(1)
    nq = Q_BLOCK
    rq = HPG * nq
    q = q_ref[0]
    gates = gt_ref[0]
    qrow = lax.broadcasted_iota(I32, (rq, 1), 0) % nq
    qpos = i * nq + qrow
    n_tiles = (i * nq + nq - 1) // SEL_KEY_TILE + 1
    qgs, o_cs, sels = [], [], []
    for g in range(N_KV):
        qg = _group_q(q, g)
        gl = slice(g * HEAD_DIM, (g + 1) * HEAD_DIM)
        lc = _dot_nt(qg, kc_ref[0, :, gl]) * SCALE + tc_ref[0, g]
        jc = lax.broadcasted_iota(I32, lc.shape, 1)
        valid_c = (qpos - (jc * CMP_STRIDE + CMP_BLOCK - 1) >= 0) & (jc < ncmp)
        pr_c = _masked_softmax(lc, valid_c)
        qgs.append(qg)
        o_cs.append(_dot(pr_c.astype(BF16), vc_ref[0, :, gl]))
        sels.append(_select_blocks(pr_c, inter_ref[...], i, nq, ns))

    def body(kt, carry):
        k0 = pl.multiple_of(kt * SEL_KEY_TILE, SEL_KEY_TILE)
        tb = jnp.minimum(i - kt * (SEL_KEY_TILE // nq), ts_ref.shape[0] - 1)
        kpos = k0 + lax.broadcasted_iota(I32, (rq, SEL_KEY_TILE), 1)
        causal = kpos <= qpos
        out = []
        for g in range(N_KV):
            m, l, acc = carry[g]
            k = kvs_ref[0, pl.ds(k0, SEL_KEY_TILE), g * HEAD_DIM:(g + 1) * HEAD_DIM]
            v = kvs_ref[0, pl.ds(k0, SEL_KEY_TILE), KV_DIM + g * HEAD_DIM:KV_DIM + (g + 1) * HEAD_DIM]
            lg = _dot_nt(qgs[g], k) * SCALE + ts_ref[tb, g]
            picked = _tile_rows(_dot(sels[g], ex_ref[:, pl.ds(k0, SEL_KEY_TILE)]))
            valid = (picked > 0.5) & causal
            lg = jnp.where(valid, lg, NEG)
            m_new = jnp.maximum(m, jnp.max(lg, axis=-1, keepdims=True))
            p = jnp.where(valid, jnp.exp(lg - m_new), 0.0)
            alpha = jnp.exp(m - m_new)
            l = alpha * l + jnp.sum(p, axis=-1, keepdims=True)
            acc = alpha * acc + _dot(p.astype(BF16), v)
            out.append((m_new, l, acc))
        return tuple(out)

    init = (jnp.full((rq, 1), NEG, F32), jnp.zeros((rq, 1), F32), jnp.zeros((rq, HEAD_DIM), F32))
    fin = lax.fori_loop(0, n_tiles, body, (init,) * N_KV)

    for g in range(N_KV):
        qg, o_c = qgs[g], o_cs[g]
        gl = slice(g * HEAD_DIM, (g + 1) * HEAD_DIM)
        o_s = fin[g][2] / fin[g][1]

        w0 = pl.multiple_of(i * nq, nq)
        kw = wp_ref[0, pl.ds(w0, nwk), gl]
        vw = wp_ref[0, pl.ds(w0, nwk), KV_DIM + g * HEAD_DIM:KV_DIM + (g + 1) * HEAD_DIM]
        lw = _dot_nt(qg, kw) * SCALE + tw_ref[g]
        jw = lax.broadcasted_iota(I32, lw.shape, 1)
        dist = WINDOW + qrow - jw
        valid_w = (i * nq - WINDOW + jw >= 0) & (dist >= 0) & (dist < WINDOW)
        o_w = _dot(_masked_softmax(lw, valid_w).astype(BF16), vw)
        _gate_out(o_c, o_s, o_w, gates, g, nq, out_ref)


def _attn_prompt(q, gates, kc, vc, kvb, rel_bias):
    b, l, _ = q.shape
    nq = Q_BLOCK
    nblk = l // nq
    ncl = kc.shape[1]
    ncmp = ncl - 1
    ns = max(-(-l // SEL_BLOCK), TOP_N)
    nwk = WINDOW + nq + (LANES - nq)
    qi = np.arange(nq)
    blk = np.arange(nblk)
    jc = np.arange(ncl)
    idx_c = _bucket_np(blk[:, None, None] * nq + qi[None, :, None] - (jc[None, None, :] * CMP_STRIDE + CMP_BLOCK - 1))
    n_ts = min(nblk, (MAX_DIST + SEL_KEY_TILE) // nq + 1)
    mt = np.arange(n_ts)
    kj = np.arange(SEL_KEY_TILE)
    idx_s = _bucket_np(mt[:, None, None] * nq + qi[None, :, None] - kj[None, None, :])
    jw = np.arange(nwk)
    idx_w = _bucket_np(WINDOW + qi[:, None] - jw[None, :])
    tc = _expand_bias(idx_c, rel_bias)
    ts = _expand_bias(idx_s, rel_bias)
    tw = _expand_bias(idx_w, rel_bias)
    inter = jnp.asarray(_inter_np(ncmp, ns)[:ncl], BF16)
    lk = -(-l // SEL_KEY_TILE) * SEL_KEY_TILE
    ex = jnp.asarray(_expand_np(lk), BF16)
    kvs = kvb[:, :, 0:2 * KV_DIM]
    if lk != l:
        kvs = jnp.pad(kvs, ((0, 0), (0, lk - l), (0, 0)))
    wp = jnp.pad(kvb[:, :, 2 * KV_DIM:4 * KV_DIM], ((0, 0), (WINDOW, nwk - WINDOW - nq), (0, 0)))
    lw = wp.shape[1]
    kern = functools.partial(_attn_prompt_kernel, ncmp=ncmp, ns=ns, nwk=nwk)
    return pl.pallas_call(
        kern, grid=(b, nblk), name="attn_prompt",
        in_specs=[pl.BlockSpec((1, nq, ATTN_DIM), lambda bi, i: (bi, i, 0)),
                  pl.BlockSpec((1, nq, LANES), lambda bi, i: (bi, i, 0)),
                  pl.BlockSpec((1, ncl, KV_DIM), lambda bi, i: (bi, 0, 0)),
                  pl.BlockSpec((1, ncl, KV_DIM), lambda bi, i: (bi, 0, 0)),
                  pl.BlockSpec((1, lk, 2 * KV_DIM), lambda bi, i: (bi, 0, 0)),
                  pl.BlockSpec((1, lw, 2 * KV_DIM), lambda bi, i: (bi, 0, 0)),
                  pl.BlockSpec((1, N_KV, HPG * nq, ncl), lambda bi, i: (i, 0, 0, 0)),
                  _const_spec(ts.shape), _const_spec(tw.shape), _const_spec(inter.shape), _const_spec(ex.shape)],
        out_specs=pl.BlockSpec((1, nq, ATTN_DIM), lambda bi, i: (bi, i, 0)),
        out_shape=jax.ShapeDtypeStruct((b, l, ATTN_DIM), F32),
        compiler_params=_cparams("arbitrary", "arbitrary"),
    )(q, gates, kc, vc, kvs, wp, tc, ts, tw, inter, ex)


def _attn_sample_kernel(pt_ref, *refs, n_pages, ns, nq, win_len):
    pages = refs[:n_pages]
    (rows_ref, wrows_ref, state_ref, q_ref, gt_ref, w1r_ref, pe_ref, w2_ref, kg_ref,
     tc_ref, ts_ref, tn_ref, tw_ref, inter_ref, ex_ref, perm_ref, out_ref, win_ref) = refs[n_pages:]
    del pt_ref
    page = pages[0].shape[1] // KV_ROWS
    cpp = page // CMP_STRIDE

    def page_rows(p, slot, g):
        return pages[p][0, pl.ds(slot * N_KV + g, page, stride=KV_ROWS), :].astype(BF16)
    past = n_pages * page
    ncl = past // CMP_STRIDE
    rq = HPG * nq
    q = q_ref[0]
    gates = gt_ref[0]
    qrow = lax.broadcasted_iota(I32, (rq, 1), 0) % nq
    cur = past // SEL_BLOCK
    pad_new = jnp.zeros((LANES - nq, HEAD_DIM), BF16)
    lane_new = lax.broadcasted_iota(I32, (rq, LANES), 1)
    valid_new = lane_new <= qrow

    win_ref[0, 0:(win_len - nq) * WIN_ROWS, :] = state_ref[0, nq * WIN_ROWS:win_len * WIN_ROWS, :]
    win_ref[0, (win_len - nq) * WIN_ROWS:win_len * WIN_ROWS, :] = wrows_ref[0]

    def new_rows(ref, nrows, slot, g):
        return jnp.concatenate([ref[0, pl.ds(slot * N_KV + g, nq, stride=nrows), :].astype(BF16), pad_new], axis=0)

    for g in range(N_KV):
        qg = _group_q(q, g)
        gl = slice(g * HEAD_DIM, (g + 1) * HEAD_DIM)

        def cmp_branch(br):
            by_s = [_dot(perm_ref[...], page_rows(p, br, g)).astype(BF16) for p in range(n_pages)]
            slabs = [jnp.concatenate([r[s * cpp:(s + 1) * cpp] for r in by_s], axis=0) for s in range(CMP_STRIDE)]
            return _compress_one(slabs, w1r_ref[br], pe_ref[br, 0:1, :], w2_ref[br])

        kc = _head_norm(cmp_branch(0), kg_ref[...]).astype(BF16)
        vc = cmp_branch(1).astype(BF16)
        lc = _dot_nt(qg, kc) * SCALE + tc_ref[g]
        jc = lax.broadcasted_iota(I32, lc.shape, 1)
        pr_c = _masked_softmax(lc, jc < ncl - 1)
        o_c = _dot(pr_c.astype(BF16), vc)
        sel = _select_blocks(pr_c, inter_ref[...], cur, nq, ns)

        picked = _tile_rows(_dot(sel, ex_ref[...]))
        lg = jnp.concatenate([_dot_nt(qg, page_rows(p, 2, g)) for p in range(n_pages)], axis=1) * SCALE + ts_ref[g]
        valid = picked > 0.5
        lg = jnp.where(valid, lg, NEG)
        k_new = new_rows(rows_ref, KV_ROWS, 2, g)
        v_new = new_rows(rows_ref, KV_ROWS, 3, g)
        ln = jnp.where(valid_new, _dot_nt(qg, k_new) * SCALE + tn_ref[g], NEG)
        m = jnp.maximum(jnp.max(lg, axis=-1, keepdims=True), jnp.max(ln, axis=-1, keepdims=True))
        p_old = jnp.where(valid, jnp.exp(lg - m), 0.0)
        p_new = jnp.where(valid_new, jnp.exp(ln - m), 0.0)
        den = jnp.sum(p_old, axis=-1, keepdims=True) + jnp.sum(p_new, axis=-1, keepdims=True)
        p_old = (p_old / den).astype(BF16)
        o_s = _dot((p_new / den).astype(BF16), v_new)
        for p in range(n_pages):
            o_s = o_s + _dot(p_old[:, p * page:(p + 1) * page], page_rows(p, 3, g))

        kw = state_ref[0, pl.ds(g, win_len, stride=WIN_ROWS), :].astype(BF16)
        vw = state_ref[0, pl.ds(N_KV + g, win_len, stride=WIN_ROWS), :].astype(BF16)
        lw = _dot_nt(qg, kw) * SCALE + tw_ref[g]
        jw = lax.broadcasted_iota(I32, lw.shape, 1)
        valid_w = jw > qrow + (win_len - WINDOW)
        lw = jnp.where(valid_w, lw, NEG)
        kwn = new_rows(wrows_ref, WIN_ROWS, 0, g)
        vwn = new_rows(wrows_ref, WIN_ROWS, 1, g)
        lwn = jnp.where(valid_new, _dot_nt(qg, kwn) * SCALE + tn_ref[g], NEG)
        m = jnp.maximum(jnp.max(lw, axis=-1, keepdims=True), jnp.max(lwn, axis=-1, keepdims=True))
        pw_old = jnp.where(valid_w, jnp.exp(lw - m), 0.0)
        pw_new = jnp.where(valid_new, jnp.exp(lwn - m), 0.0)
        den = jnp.sum(pw_old, axis=-1, keepdims=True) + jnp.sum(pw_new, axis=-1, keepdims=True)
        o_w = _dot((pw_old / den).astype(BF16), vw) + _dot((pw_new / den).astype(BF16), vwn)
        _gate_out(o_c, o_s, o_w, gates, g, nq, out_ref)


def _attn_sample(q, gates, rows, wrows, cache, state, page_table, w1r, pe_hid, w2, kc_gain, rel_bias):
    nb, nq, _ = q.shape
    n_pages = page_table.shape[1]
    page = cache.shape[1] // KV_ROWS
    past = n_pages * page
    win_len = state.shape[1] // WIN_ROWS
    t_all = past + nq
    ns = max(-(-t_all // SEL_BLOCK), TOP_N)
    ncl = past // CMP_STRIDE
    qi = np.arange(nq)
    idx_c = _bucket_np(past + qi[:, None] - (np.arange(ncl)[None, :] * CMP_STRIDE + CMP_BLOCK - 1))
    idx_s = _bucket_np(past + qi[:, None] - np.arange(past)[None, :])
    idx_n = _bucket_np(qi[:, None] - np.arange(LANES)[None, :])
    idx_w = _bucket_np(win_len + qi[:, None] - np.arange(win_len)[None, :])
    tc = _expand_bias(idx_c, rel_bias)
    ts = _expand_bias(idx_s, rel_bias)
    tn = _expand_bias(idx_n, rel_bias)
    tw = _expand_bias(idx_w, rel_bias)
    inter = jnp.asarray(_inter_np(ncl - 1, ns)[:ncl], BF16)
    ex = jnp.asarray(_expand_np(past), BF16)
    tok = np.arange(page)
    perm_np = ((tok % CMP_STRIDE) * (page // CMP_STRIDE) + tok // CMP_STRIDE)[None, :] == tok[:, None]
    perm = jnp.asarray(perm_np.astype(np.float32), BF16)
    kern = functools.partial(_attn_sample_kernel, n_pages=n_pages, ns=ns, nq=nq, win_len=win_len)
    page_specs = [pl.BlockSpec((1,) + cache.shape[1:], functools.partial(lambda bi, pt, k: (pt[bi, k], 0, 0), k=k))
                  for k in range(n_pages)]
    per_b = lambda shape: pl.BlockSpec((1,) + shape, lambda bi, pt: (bi, 0, 0))
    cst = lambda a: pl.BlockSpec(a.shape, lambda bi, pt: (0,) * a.ndim, pipeline_mode=pl.Buffered(1))
    grid_spec = pltpu.PrefetchScalarGridSpec(
        num_scalar_prefetch=1, grid=(nb,),
        in_specs=page_specs + [per_b((nq * KV_ROWS, LANES)), per_b((nq * WIN_ROWS, LANES)),
                               per_b((win_len * WIN_ROWS, LANES)),
                               per_b((nq, ATTN_DIM)), per_b((nq, LANES)),
                               cst(w1r), cst(pe_hid), cst(w2), cst(kc_gain),
                               cst(tc), cst(ts), cst(tn), cst(tw), cst(inter), cst(ex), cst(perm)],
        out_specs=[per_b((nq, ATTN_DIM)), per_b((win_len * WIN_ROWS, LANES))])
    return pl.pallas_call(
        kern, grid_spec=grid_spec, name="attn_sample",
        out_shape=[jax.ShapeDtypeStruct((nb, nq, ATTN_DIM), F32),
                   jax.ShapeDtypeStruct((nb, win_len * WIN_ROWS, LANES), F32)],
        compiler_params=_cparams("arbitrary"),
    )(page_table, *([cache] * n_pages), rows, wrows, state, q, gates, w1r, pe_hid, w2, kc_gain,
      tc, ts, tn, tw, inter, ex, perm)


def _merge_kernel(attn_ref, cn_ref, x_ref, ga_ref, w_ref, h_ref):
    a = _rms(attn_ref[...], ga_ref[...]).astype(BF16)
    ac = jnp.concatenate([a, cn_ref[...]], axis=1)
    h_ref[...] = x_ref[...] + _dot(ac, w_ref[...])


def _merge(attn, cn, x2, gain_a, w_out):
    n = x2.shape[0]
    tm = _token_tile(n)
    row = lambda w: pl.BlockSpec((tm, w), lambda i: (i, 0))
    return pl.pallas_call(
        _merge_kernel, grid=(n // tm,), name="merge",
        in_specs=[row(ATTN_DIM), row(CONV_DIM), row(D_MODEL), _const_spec((1, ATTN_DIM)), _const_spec(w_out.shape)],
        out_specs=row(D_MODEL),
        out_shape=jax.ShapeDtypeStruct((n, D_MODEL), F32),
        compiler_params=_cparams("arbitrary"),
    )(attn, cn, x2, gain_a, w_out)


def _top_rows(v, k, payload=None):
    nrow = v.shape[0]
    rid = lax.broadcasted_iota(I32, v.shape, 0).astype(F32)
    vals, ids = [], []
    for _ in range(k):
        m = jnp.max(v, axis=0, keepdims=True)
        am = jnp.min(jnp.where(v == m, rid, float(nrow)), axis=0, keepdims=True)
        hit = rid == am
        vals.append(m)
        if payload is None:
            ids.append(am)
        else:
            ids.append(jnp.max(jnp.where(hit, payload, -1.0), axis=0, keepdims=True))
        v = jnp.where(hit, -jnp.inf, v)
    return jnp.concatenate(vals, axis=0), jnp.concatenate(ids, axis=0)


def _route_kernel(h_ref, g2_ref, wq_ref, keys_ref, n2_ref, eidx_ref, gw_ref, qh_ref):
    n2 = _rms(h_ref[...], g2_ref[...])
    n2_ref[...] = n2
    qh_ref[...] = _dot(n2.astype(BF16), wq_ref[...])
    half = PEER_DK // 2

    def head(r, carry):
        sv, si = [], []
        for hf in range(2):
            c0 = pl.multiple_of(r * PEER_DK + hf * half, half)
            qc = qh_ref[:, pl.ds(c0, half)].astype(BF16)
            st = _dot_nt(keys_ref[hf], qc)
            v, ix = _top_rows(st, PEER_TOPK)
            sv.append(v)
            si.append(ix)
        sub = lax.broadcasted_iota(I32, (SUBLANES, st.shape[1]), 0)
        cand, cidx = [], []
        a = 0
        while PEER_TOPK // (a + 1) > 1:
            nb = PEER_TOPK // (a + 1)
            for b0 in range(0, nb, SUBLANES):
                s = sv[0][a:a + 1] + sv[1][b0:b0 + SUBLANES]
                cand.append(s if b0 + SUBLANES <= nb else jnp.where(sub < nb - b0, s, -jnp.inf))
                cidx.append(si[0][a:a + 1] * float(N_KEYS) + si[1][b0:b0 + SUBLANES])
            a += 1
        assert PEER_TOPK - a == SUBLANES
        cand.append(sv[0][a:PEER_TOPK] + sv[1][0:1])
        cidx.append(si[0][a:PEER_TOPK] * float(N_KEYS) + si[1][0:1])
        fv, fe = _top_rows(jnp.concatenate(cand, axis=0), PEER_TOPK, payload=jnp.concatenate(cidx, axis=0))
        e = jnp.exp(fv - fv[0:1])
        gw_ref[r] = e / jnp.sum(e, axis=0, keepdims=True)
        eidx_ref[r] = fe.astype(I32)
        return carry

    lax.fori_loop(0, PEER_HEADS, head, 0)


def _route(h, g2, wq, keys):
    n = h.shape[0]
    tm = _token_tile(n)
    row = lambda w: pl.BlockSpec((tm, w), lambda i: (i, 0))
    hk = pl.BlockSpec((PEER_HEADS, PEER_TOPK, tm), lambda i: (0, 0, i))
    return pl.pallas_call(
        _route_kernel, grid=(n // tm,), name="peer_route",
        in_specs=[row(D_MODEL), _const_spec((1, D_MODEL)), _const_spec(wq.shape), _const_spec(keys.shape)],
        out_specs=[row(D_MODEL), hk, hk],
        out_shape=[jax.ShapeDtypeStruct((n, D_MODEL), F32),
                   jax.ShapeDtypeStruct((PEER_HEADS, PEER_TOPK, n), I32),
                   jax.ShapeDtypeStruct((PEER_HEADS, PEER_TOPK, n), F32)],
        scratch_shapes=[pltpu.VMEM((tm, PEER_HEADS * PEER_DK), F32)],
        compiler_params=_cparams("arbitrary"),
    )(h, g2, wq, keys)


_BITREV = (0, 4, 2, 6, 1, 5, 3, 7)


def _sublane_sums(vs):
    sub = lax.broadcasted_iota(I32, (SUBLANES, LANES), 0)
    cur = [vs[j] for j in _BITREV]
    sh = SUBLANES // 2
    while sh >= 1:
        low = (sub & sh) == 0
        nxt = []
        for j in range(0, len(cur), 2):
            a, b = cur[j], cur[j + 1]
            nxt.append(jnp.where(low, a, pltpu.roll(b, sh, 0)) + jnp.where(low, pltpu.roll(a, SUBLANES - sh, 0), b))
        cur = nxt
        sh //= 2
    return cur[0]


def _peer_kernel(idx_cur_ref, idx_nxt_ref, x_ref, h_ref, gw_ref, tab_ref, out_ref, *scratch, nsteps):
    bufs, sem_ref = scratch[:PEER_PHASES], scratch[PEER_PHASES]
    i = pl.program_id(0)
    tt = PEER_TOKENS
    nrow = tt * PEER_PAIRS
    half = tab_ref.shape[1] // 2
    lane = lax.broadcasted_iota(I32, (SUBLANES, gw_ref.shape[1]), 1)
    tok0 = (i * PEER_PHASES * tt) % gw_ref.shape[1]

    def fetch(idx_row, j, dst_rows, sem, queue):
        pltpu.make_async_copy(tab_ref.at[idx_row[j]], dst_rows.at[j], sem).start(priority=queue)

    def wait_rows(dst, sem):
        pltpu.make_async_copy(tab_ref.at[pl.ds(0, nrow)], dst, sem).wait()

    def half_step(src, tok_off, idx_ref, idx_off, dst, dst_sem):
        def token(t, carry):
            x = x_ref[tok_off + t]
            x_lo = x[0:SUBLANES]
            x_hi = x[SUBLANES:2 * SUBLANES]
            my_lane = lane == tok0 + tok_off + t
            row0 = pl.multiple_of(t * PEER_PAIRS, PEER_PAIRS)
            src_rows = src.at[pl.ds(row0, PEER_PAIRS)]
            if dst is not None:
                dst_rows = dst.at[pl.ds(row0, PEER_PAIRS)]
                idx_row = idx_ref.at[0, 0, pl.ds(idx_off + row0, PEER_PAIRS)]
            acc_lo = jnp.zeros((SUBLANES, LANES), F32)
            acc_hi = jnp.zeros((SUBLANES, LANES), F32)
            for grp in range(PEER_PAIRS // SUBLANES):
                base = grp * SUBLANES
                prods = []
                for k in range(SUBLANES):
                    if dst is not None:
                        fetch(idx_row, base + k, dst_rows, dst_sem, k % 2)
                    u = src_rows[base + k, 0:half, :].astype(F32)
                    prods.append(u[0:SUBLANES] * x_lo + u[SUBLANES:2 * SUBLANES] * x_hi)
                act = jnp.sum(_sublane_sums(prods), axis=-1, keepdims=True)
                gsub = jnp.sum(jnp.where(my_lane, gw_ref[base:base + SUBLANES, :], 0.0), axis=-1, keepdims=True)
                coef = jnp.broadcast_to(_gelu(act) * gsub, (SUBLANES, LANES))
                for k in range(SUBLANES):
                    v = src_rows[base + k, half:2 * half, :].astype(F32)
                    ck = jnp.broadcast_to(coef[k:k + 1, :], (SUBLANES, LANES))
                    acc_lo = acc_lo + ck * v[0:SUBLANES]
                    acc_hi = acc_hi + ck * v[SUBLANES:2 * SUBLANES]
            out_ref[tok_off + t] = h_ref[tok_off + t] + jnp.concatenate([acc_lo, acc_hi], axis=0)
            return carry

        lax.fori_loop(0, tt, token, 0)

    ahead = PEER_PHASES // 2

    @pl.when(i == 0)
    def _():
        for k in range(ahead):
            def prime(j, carry, k=k):
                row = idx_cur_ref.at[0, 0, pl.ds(k * nrow, nrow)]
                fetch(row, 2 * j, bufs[k], sem_ref.at[k], 0)
                fetch(row, 2 * j + 1, bufs[k], sem_ref.at[k], 1)
                return carry
            lax.fori_loop(0, nrow // 2, prime, 0)

    for k in range(PEER_PHASES):
        wait_rows(bufs[k], sem_ref.at[k])
        nk = (k + ahead) % PEER_PHASES
        if k + ahead < PEER_PHASES:
            half_step(bufs[k], k * tt, idx_cur_ref, nk * nrow, bufs[nk], sem_ref.at[nk])
        else:
            @pl.when(i < nsteps - 1)
            def _(k=k, nk=nk):
                half_step(bufs[k], k * tt, idx_nxt_ref, nk * nrow, bufs[nk], sem_ref.at[nk])

            @pl.when(i == nsteps - 1)
            def _(k=k):
                half_step(bufs[k], k * tt, None, 0, None, None)


def _peer(n2, h, eidx, gw, tab):
    n = n2.shape[0]
    tstep = PEER_PHASES * PEER_TOKENS
    nsteps = n // tstep
    sub = D_MODEL // LANES
    x3 = n2.reshape(n, sub, LANES)
    h3 = h.reshape(n, sub, LANES)
    idx3 = eidx.reshape(nsteps, 1, tstep * PEER_PAIRS)
    tok = pl.BlockSpec((tstep, sub, LANES), lambda i: (i, 0, 0))
    smem = lambda f: pl.BlockSpec((1, 1, tstep * PEER_PAIRS), f, memory_space=pltpu.SMEM)
    gw_lanes = LANES if n % LANES == 0 else n
    rows = pltpu.VMEM((PEER_TOKENS * PEER_PAIRS,) + tab.shape[1:], tab.dtype)
    out = pl.pallas_call(
        functools.partial(_peer_kernel, nsteps=nsteps), grid=(nsteps,), name="peer_ffn",
        in_specs=[smem(lambda i: (i, 0, 0)), smem(lambda i: (jnp.minimum(i + 1, nsteps - 1), 0, 0)),
                  tok, tok,
                  pl.BlockSpec((PEER_PAIRS, gw_lanes), lambda i: (0, (i * tstep) // gw_lanes)),
                  pl.BlockSpec(memory_space=pl.ANY)],
        out_specs=tok,
        out_shape=jax.ShapeDtypeStruct((n, sub, LANES), F32),
        scratch_shapes=[rows] * PEER_PHASES + [pltpu.SemaphoreType.DMA((PEER_PHASES,))],
        compiler_params=_cparams("arbitrary"),
    )(idx3, idx3, x3, h3, gw, tab)
    return out.reshape(n, D_MODEL)


def _ple_kernel(h_ref, p_ref, g3_ref, wg_ref, wp_ref, y_ref):
    h = h_ref[...]
    gate = _sigmoid(_dot(_rms(h, g3_ref[...]).astype(BF16), wg_ref[...]))
    y_ref[...] = h + gate * _dot(p_ref[...].astype(BF16), wp_ref[...])


def _ple(h, p, g3, w_gate, w_ple):
    n = h.shape[0]
    tm = _token_tile(n)
    row = lambda w: pl.BlockSpec((tm, w), lambda i: (i, 0))
    return pl.pallas_call(
        _ple_kernel, grid=(n // tm,), name="ple",
        in_specs=[row(D_MODEL), row(PLE_DIM), _const_spec((1, D_MODEL)), _const_spec(w_gate.shape),
                  _const_spec(w_ple.shape)],
        out_specs=row(D_MODEL),
        out_shape=jax.ShapeDtypeStruct((n, D_MODEL), F32),
        compiler_params=_cparams("arbitrary"),
    )(h, p, g3, w_gate, w_ple)


def _tail(attn, cn, x2, p2, wts):
    h = _merge(attn, cn, x2, wts["gain_a"], wts["w_out"])
    n2, eidx_t, gw_t = _route(h, wts["g2"], wts["wq"], wts["keys"])
    n = h.shape[0]
    eidx = eidx_t.reshape(PEER_PAIRS, n).T
    gw = gw_t.reshape(PEER_PAIRS, n)
    h = _peer(n2, h, eidx, gw, wts["tab"])
    return _ple(h, p2, wts["g3"], wts["w_gate"], wts["w_ple"])


def kernel(x_prompt, x_sample, p_prompt, p_sample, cache_kv, state_win, state_conv, page_table, rel_bias, norm1,
           w_in, qk_gain, cmp_pe, cmp_w1, cmp_w2, conv_w, conv_b, out_gain, w_out, norm2, peer_wq, peer_keys,
           peer_u, peer_v, norm3, ple_gate, ple_proj):
    depth = norm1.shape[0]
    assert depth == 1
    l0 = 0
    b, l, d = x_prompt.shape
    nb, nq, _ = x_sample.shape
    n_pool, page = cache_kv.shape[1], cache_kv.shape[2]
    win_len = state_win.shape[2]

    w = w_in[l0]
    c_gt = ATTN_DIM + 6 * KV_DIM
    c_cv = c_gt + N_BRANCH * N_HEADS
    w_qkv = w[:, 0:c_gt].astype(BF16)
    w_cv = jnp.concatenate([w[:, c_cv:], jnp.pad(w[:, c_gt:c_cv], ((0, 0), (0, LANES - N_BRANCH * N_HEADS)))],
                           axis=1).astype(BF16)
    g1 = norm1[l0].reshape(1, d)
    w1 = cmp_w1[l0]
    half_k = CMP_STRIDE * HEAD_DIM
    w1r = jnp.concatenate([w1[:, 0:half_k], w1[:, half_k:2 * half_k]], axis=2).astype(BF16)
    w2 = cmp_w2[l0].astype(BF16)
    pe_hid = _pe_hid(cmp_pe[l0], w1)
    kc_gain = qk_gain[l0, 1].reshape(1, HEAD_DIM)
    wts = dict(
        gain_a=out_gain[l0, 0:ATTN_DIM].reshape(1, ATTN_DIM), w_out=w_out[l0].astype(BF16),
        g2=norm2[l0].reshape(1, d), wq=peer_wq[l0].astype(BF16), keys=peer_keys[l0].astype(BF16),
        tab=jnp.concatenate([peer_u[l0].astype(BF16).reshape(-1, d // LANES, LANES),
                             peer_v[l0].astype(BF16).reshape(-1, d // LANES, LANES)], axis=1),
        g3=norm3[l0].reshape(1, d), w_gate=ple_gate[l0].astype(BF16), w_ple=ple_proj[l0].astype(BF16))
    gain_c = out_gain[l0, ATTN_DIM:].reshape(1, CONV_DIM)
    cw = conv_w[l0]
    cb = conv_b[l0].reshape(1, CONV_DIM)

    xp = x_prompt.reshape(b * l, d)
    q, rows, wrows, kvb, bu, gates = _mix(xp, g1, w_qkv, w_cv, qk_gain[l0])
    kc, vc = _compress_prompt(rows.reshape(b, l * KV_ROWS, LANES), w1r, pe_hid, w2, kc_gain)
    attn = _attn_prompt(q.reshape(b, l, ATTN_DIM), gates.reshape(b, l, LANES), kc, vc,
                        kvb.reshape(b, l, 4 * KV_DIM), rel_bias)
    cn, cst_p = _conv(bu.reshape(b, l, 2 * CONV_DIM), jnp.zeros((b, CONV_WIDTH - 1, CONV_DIM), F32), cw, cb, gain_c)
    y_p = _tail(attn.reshape(b * l, ATTN_DIM), cn.reshape(b * l, CONV_DIM), xp,
                p_prompt[l0].reshape(b * l, PLE_DIM), wts)
    wl = min(WINDOW, l)
    kv_p = rows.reshape(1, b, l, 4, N_KV, HEAD_DIM)
    win_p = wrows.reshape(b, l, 2, N_KV, HEAD_DIM)[None, :, l - wl:]
    conv_p = cst_p[None]

    xs = x_sample.reshape(nb * nq, d)
    q, rows, wrows, kvb, bu, gates = _mix(xs, g1, w_qkv, w_cv, qk_gain[l0])
    attn, win_s = _attn_sample(q.reshape(nb, nq, ATTN_DIM), gates.reshape(nb, nq, LANES),
                               rows.reshape(nb, nq * KV_ROWS, LANES), wrows.reshape(nb, nq * WIN_ROWS, LANES),
                               cache_kv[l0].reshape(n_pool, page * KV_ROWS, LANES),
                               state_win[l0].reshape(nb, win_len * WIN_ROWS, LANES), page_table,
                               w1r, pe_hid, w2, kc_gain, rel_bias)
    cn, cst_s = _conv(bu.reshape(nb, nq, 2 * CONV_DIM), state_conv[l0], cw, cb, gain_c)
    y_s = _tail(attn.reshape(nb * nq, ATTN_DIM), cn.reshape(nb * nq, CONV_DIM), xs,
                p_sample[l0].reshape(nb * nq, PLE_DIM), wts)
    kv_s = rows.reshape(1, nb, nq, 4, N_KV, HEAD_DIM)
    win_s = win_s.reshape(1, nb, win_len, 2, N_KV, HEAD_DIM)
    conv_s = cst_s[None]

    return (y_p.reshape(b, l, d), y_s.reshape(nb, nq, d), kv_p, win_p, conv_p, kv_s, win_s, conv_s)
```

```python
import functools
import math

import numpy as np
import jax
import jax.numpy as jnp
from jax import lax
from jax.experimental import pallas as pl
from jax.experimental.pallas import tpu as pltpu

F32 = jnp.float32
BF16 = jnp.bfloat16
I32 = jnp.int32

D_MODEL = 2048
HEAD_DIM = 128
N_HEADS = 8
N_KV = 2
HPG = N_HEADS // N_KV
ATTN_DIM = N_HEADS * HEAD_DIM
KV_DIM = N_KV * HEAD_DIM
CONV_DIM = D_MODEL - ATTN_DIM
CONV_WIDTH = 3
CMP_BLOCK = 32
CMP_STRIDE = 16
CMP_HID = 2 * HEAD_DIM
SEL_BLOCK = 64
TOP_N = 8
WINDOW = 512
Q_BLOCK = 64
N_BUCKETS = 32
MAX_DIST = 128
N_KEYS = 128
PEER_HEADS = 8
PEER_TOPK = 16
PEER_DK = 256
PLE_DIM = 256
N_BRANCH = 3
EPS = 1e-6
NEG = -1e30
FORCE = 1e4
SCALE = HEAD_DIM ** -0.5

LANES = 128
SUBLANES = 8
VMEM_LIMIT = 56 * 1024 * 1024
TOKEN_TILE = 256
PEER_TOKENS = 8
PEER_PAIRS = PEER_HEADS * PEER_TOPK
PEER_PHASES = 4
SEL_KEY_TILE = 512
KV_ROWS = 4 * KV_DIM // LANES
WIN_ROWS = 2 * KV_DIM // LANES


def _cparams(*sem):
    return pltpu.CompilerParams(dimension_semantics=sem, vmem_limit_bytes=VMEM_LIMIT)


def _const_spec(shape):
    nd = len(shape)
    return pl.BlockSpec(shape, lambda *_: (0,) * nd, pipeline_mode=pl.Buffered(1))


def _rms(x, g):
    ms = jnp.mean(x * x, axis=-1, keepdims=True)
    return x * lax.rsqrt(ms + EPS) * g


def _gelu(x):
    c = math.sqrt(2.0 / math.pi)
    return x * (0.5 * (1.0 + jnp.tanh(c * (x + 0.044715 * (x * x * x)))))


def _sigmoid(x):
    return 1.0 / (1.0 + jnp.exp(-x))


def _dot(a, b):
    return jnp.dot(a, b, preferred_element_type=F32)


def _dot_nt(a, b):
    return lax.dot_general(a, b, (((1,), (1,)), ((), ())), preferred_element_type=F32)


def _dot_split(a, b):
    hi = a.astype(BF16)
    r1 = a - hi.astype(F32)
    mid = r1.astype(BF16)
    lo = (r1 - mid.astype(F32)).astype(BF16)
    return _dot(hi, b) + _dot(mid, b) + _dot(lo, b)


def _bucket_np(dist):
    n = np.maximum(dist, 0)
    exact = N_BUCKETS // 2
    nf = np.maximum(n, 1).astype(np.float32)
    large = exact + (np.log(nf / np.float32(exact)) / np.float32(math.log(MAX_DIST / exact))
                     * np.float32(N_BUCKETS - exact)).astype(np.int32)
    return np.where(n < exact, n, np.minimum(large, N_BUCKETS - 1)).astype(np.int32)


def _bias_expand_kernel(tab_ref, idx_ref, out_ref):
    h = pl.program_id(0)
    idx = idx_ref[...]
    acc = jnp.zeros(idx.shape, F32)
    for b in range(N_BUCKETS):
        acc = jnp.where(idx == b, tab_ref[b, h], acc)
    out_ref[0] = acc


def _expand_bias(idx_np, rel_bias):
    lead = idx_np.shape[:-2]
    nq, nk = idx_np.shape[-2:]
    rows = int(np.prod(lead, dtype=np.int64)) * nq
    idx2 = jnp.asarray(idx_np.reshape(rows, nk))
    tr = rows
    while tr * nk * 4 > (1 << 20) and tr % 16 == 0:
        tr //= 2
    out = pl.pallas_call(
        _bias_expand_kernel, name="bias_expand",
        grid=(N_HEADS, rows // tr),
        in_specs=[pl.BlockSpec(memory_space=pltpu.SMEM),
                  pl.BlockSpec((tr, nk), lambda h, r: (r, 0))],
        out_specs=pl.BlockSpec((1, tr, nk), lambda h, r: (h, r, 0)),
        out_shape=jax.ShapeDtypeStruct((N_HEADS, rows, nk), F32),
        compiler_params=_cparams("arbitrary", "arbitrary"),
    )(rel_bias.astype(F32), idx2)
    nl = len(lead)
    out = out.reshape((N_KV, HPG) + lead + (nq, nk))
    perm = tuple(range(2, 2 + nl)) + (0, 1, 2 + nl, 3 + nl)
    out = out.transpose(perm)
    return out.reshape(lead + (N_KV, HPG * nq, nk))


def _head_norm(z, gain):
    ms = jnp.mean(z * z, axis=-1, keepdims=True)
    return z * lax.rsqrt(ms + EPS) * gain


def _mix_qkv_kernel(x_ref, g1_ref, w_ref, qg_ref, q_ref, rows_ref, wrows_ref, kvb_ref):
    n = _rms(x_ref[...], g1_ref[...]).astype(BF16)
    z = _dot(n, w_ref[...])
    hd = HEAD_DIM
    for h in range(N_HEADS):
        q_ref[:, h * hd:(h + 1) * hd] = _head_norm(z[:, h * hd:(h + 1) * hd], qg_ref[0:1, :]).astype(BF16)
    kv = ATTN_DIM
    tm = z.shape[0]

    def put_row(slot, g, val):
        rows_ref[pl.ds(slot * N_KV + g, tm, stride=KV_ROWS), :] = val

    def put_wrow(slot, g, val):
        wrows_ref[pl.ds(slot * N_KV + g, tm, stride=WIN_ROWS), :] = val

    for g in range(N_KV):
        put_row(0, g, z[:, kv + g * hd:kv + (g + 1) * hd])
        put_row(1, g, z[:, kv + KV_DIM + g * hd:kv + KV_DIM + (g + 1) * hd])
        c = kv + 2 * KV_DIM + g * hd
        ks = _head_norm(z[:, c:c + hd], qg_ref[2:3, :])
        put_row(2, g, ks)
        kvb_ref[:, g * hd:(g + 1) * hd] = ks.astype(BF16)
        c = kv + 3 * KV_DIM + g * hd
        put_row(3, g, z[:, c:c + hd])
        kvb_ref[:, KV_DIM + g * hd:KV_DIM + (g + 1) * hd] = z[:, c:c + hd].astype(BF16)
        c = kv + 4 * KV_DIM + g * hd
        kw = _head_norm(z[:, c:c + hd], qg_ref[3:4, :])
        put_wrow(0, g, kw)
        kvb_ref[:, 2 * KV_DIM + g * hd:2 * KV_DIM + (g + 1) * hd] = kw.astype(BF16)
        c = kv + 5 * KV_DIM + g * hd
        put_wrow(1, g, z[:, c:c + hd])
        kvb_ref[:, 3 * KV_DIM + g * hd:3 * KV_DIM + (g + 1) * hd] = z[:, c:c + hd].astype(BF16)


def _mix_cv_kernel(x_ref, g1_ref, w_ref, bu_ref, gates_ref):
    n = _rms(x_ref[...], g1_ref[...]).astype(BF16)
    z = _dot(n, w_ref[...])
    c = CONV_DIM
    bu_ref[:, 0:c] = z[:, 0:c]
    bu_ref[:, c:2 * c] = z[:, c:2 * c] * z[:, 2 * c:3 * c]
    gates_ref[...] = _sigmoid(z[:, 3 * c:3 * c + LANES])


def _token_tile(n):
    return TOKEN_TILE if n % TOKEN_TILE == 0 else n


def _mix(x2, g1, w_qkv, w_cv, qk_gain):
    n = x2.shape[0]
    tm = _token_tile(n)
    grid = (n // tm,)
    row = lambda w: pl.BlockSpec((tm, w), lambda i: (i, 0))
    q, rows, wrows, kvb = pl.pallas_call(
        _mix_qkv_kernel, grid=grid, name="mix_qkv",
        in_specs=[row(D_MODEL), _const_spec((1, D_MODEL)), _const_spec(w_qkv.shape), _const_spec((4, HEAD_DIM))],
        out_specs=[row(ATTN_DIM), pl.BlockSpec((tm * KV_ROWS, LANES), lambda i: (i, 0)),
                   pl.BlockSpec((tm * WIN_ROWS, LANES), lambda i: (i, 0)), row(4 * KV_DIM)],
        out_shape=[jax.ShapeDtypeStruct((n, ATTN_DIM), BF16), jax.ShapeDtypeStruct((n * KV_ROWS, LANES), F32),
                   jax.ShapeDtypeStruct((n * WIN_ROWS, LANES), F32), jax.ShapeDtypeStruct((n, 4 * KV_DIM), BF16)],
        compiler_params=_cparams("arbitrary"),
    )(x2, g1, w_qkv, qk_gain)
    bu, gates = pl.pallas_call(
        _mix_cv_kernel, grid=grid, name="mix_cv",
        in_specs=[row(D_MODEL), _const_spec((1, D_MODEL)), _const_spec(w_cv.shape)],
        out_specs=[row(2 * CONV_DIM), row(LANES)],
        out_shape=[jax.ShapeDtypeStruct((n, 2 * CONV_DIM), F32), jax.ShapeDtypeStruct((n, LANES), F32)],
        compiler_params=_cparams("arbitrary"),
    )(x2, g1, w_cv)
    return q, rows, wrows, kvb, bu, gates


def _conv_kernel(bu_ref, prev_ref, w_ref, b_ref, g_ref, cn_ref, cst_ref, carry_ref):
    @pl.when(pl.program_id(1) == 0)
    def _():
        carry_ref[...] = prev_ref[0]

    c = CONV_DIM
    bg = bu_ref[0, :, 0:c]
    u = bu_ref[0, :, c:2 * c]
    tl = u.shape[0]
    row = lax.broadcasted_iota(I32, u.shape, 0)
    c0 = carry_ref[0:1, :]
    c1 = carry_ref[1:2, :]
    u1 = jnp.where(row == 0, c1, pltpu.roll(u, 1, 0))
    u2 = jnp.where(row == 0, c0, jnp.where(row == 1, c1, pltpu.roll(u, 2, 0)))
    y = w_ref[0:1, :] * u2 + w_ref[1:2, :] * u1 + w_ref[2:3, :] * u
    conv = bg * (y + b_ref[...])
    cn_ref[0] = _rms(conv, g_ref[...]).astype(BF16)
    last = u[tl - 2:tl, :]
    carry_ref[...] = last
    cst_ref[0] = last


def _conv(bu, prev, conv_w, conv_b, gain_c):
    b, l, _ = bu.shape
    tl = 512 if l % 512 == 0 else l
    c = CONV_DIM
    return pl.pallas_call(
        _conv_kernel, grid=(b, l // tl), name="short_conv",
        in_specs=[pl.BlockSpec((1, tl, 2 * c), lambda i, t: (i, t, 0)),
                  pl.BlockSpec((1, CONV_WIDTH - 1, c), lambda i, t: (i, 0, 0)),
                  _const_spec((CONV_WIDTH, c)), _const_spec((1, c)), _const_spec((1, c))],
        out_specs=[pl.BlockSpec((1, tl, c), lambda i, t: (i, t, 0)),
                   pl.BlockSpec((1, CONV_WIDTH - 1, c), lambda i, t: (i, 0, 0))],
        out_shape=[jax.ShapeDtypeStruct((b, l, c), BF16), jax.ShapeDtypeStruct((b, CONV_WIDTH - 1, c), F32)],
        scratch_shapes=[pltpu.VMEM((CONV_WIDTH - 1, c), F32)],
        compiler_params=_cparams("arbitrary", "arbitrary"),
    )(bu, prev, conv_w, conv_b, gain_c)


def _pe_hid_kernel(pe_ref, w_ref, out_ref):
    out_ref[0] = _dot(pe_ref[0].astype(BF16), w_ref[0].astype(BF16))


def _pe_hid(cmp_pe, cmp_w1):
    k = CMP_BLOCK * HEAD_DIM
    pe = jnp.broadcast_to(cmp_pe.reshape(2, 1, k), (2, SUBLANES, k))
    return pl.pallas_call(
        _pe_hid_kernel, grid=(2,), name="pe_hid",
        in_specs=[pl.BlockSpec((1, SUBLANES, k), lambda i: (i, 0, 0)),
                  pl.BlockSpec((1, k, CMP_HID), lambda i: (i, 0, 0))],
        out_specs=pl.BlockSpec((1, SUBLANES, CMP_HID), lambda i: (i, 0, 0)),
        out_shape=jax.ShapeDtypeStruct((2, SUBLANES, CMP_HID), F32),
        compiler_params=_cparams("arbitrary"),
    )(pe, cmp_w1)


def _compress_one(slabs, w1r, pe_hid, w2):
    a = jnp.concatenate(slabs, axis=1)
    nch = a.shape[0]
    part = _dot(a, w1r)
    p0 = part[:, 0:CMP_HID]
    p1 = part[:, CMP_HID:2 * CMP_HID]
    hid = pe_hid + p0 + pltpu.roll(p1, nch - 1, 0)
    return _dot(_gelu(hid).astype(BF16), w2)


def _compress_prompt_kernel(rows_ref, w1r_ref, pe_ref, w2_ref, kg_ref, kc_ref, vc_ref):
    nch = rows_ref.shape[1] // (KV_ROWS * CMP_STRIDE)
    for br in range(2):
        for g in range(N_KV):
            j = br * N_KV + g
            slabs = [rows_ref[0, pl.ds(s * KV_ROWS + j, nch, stride=KV_ROWS * CMP_STRIDE), :].astype(BF16)
                     for s in range(CMP_STRIDE)]
            out = _compress_one(slabs, w1r_ref[br], pe_ref[br, 0:1, :], w2_ref[br])
            if br == 0:
                kc_ref[0, :, g * HEAD_DIM:(g + 1) * HEAD_DIM] = _head_norm(out, kg_ref[...]).astype(BF16)
            else:
                vc_ref[0, :, g * HEAD_DIM:(g + 1) * HEAD_DIM] = out.astype(BF16)


def _compress_prompt(rows, w1r, pe_hid, w2, kc_gain):
    b = rows.shape[0]
    t = rows.shape[1] // KV_ROWS
    nch = t // CMP_STRIDE
    out = jax.ShapeDtypeStruct((b, nch, KV_DIM), BF16)
    return pl.pallas_call(
        _compress_prompt_kernel, grid=(b,), name="compress_prompt",
        in_specs=[pl.BlockSpec((1, t * KV_ROWS, LANES), lambda i: (i, 0, 0)),
                  _const_spec(w1r.shape), _const_spec(pe_hid.shape), _const_spec(w2.shape),
                  _const_spec((1, HEAD_DIM))],
        out_specs=[pl.BlockSpec((1, nch, KV_DIM), lambda i: (i, 0, 0))] * 2,
        out_shape=[out, out],
        compiler_params=_cparams("arbitrary"),
    )(rows, w1r, pe_hid, w2, kc_gain)


def _group_q(q, g):
    return jnp.concatenate([q[:, (g * HPG + r) * HEAD_DIM:(g * HPG + r + 1) * HEAD_DIM] for r in range(HPG)], axis=0)


def _tile_rows(x):
    return jnp.concatenate([x] * HPG, axis=0)


def _masked_softmax(logits, valid):
    l = jnp.where(valid, logits, NEG)
    m = jnp.max(l, axis=-1, keepdims=True)
    p = jnp.where(valid, jnp.exp(l - m), 0.0)
    s = jnp.sum(p, axis=-1, keepdims=True)
    return p / jnp.where(s > 0.0, s, 1.0)


def _select_blocks(pr_c, inter, cur, nq, ns):
    prsum = pr_c[0:nq]
    for r in range(1, HPG):
        prsum = prsum + pr_c[r * nq:(r + 1) * nq]
    imp = _dot_split(prsum, inter)
    sid = lax.broadcasted_iota(I32, imp.shape, 1)
    imp = jnp.where((sid == 0) | (sid == cur) | (sid == cur - 1), FORCE, imp)
    imp = jnp.where(sid > cur, -jnp.inf, imp)
    cnt = jnp.zeros(imp.shape, F32)
    for j in range(ns):
        col = imp[:, j:j + 1]
        ahead = (col > imp) | ((col == imp) & (sid > j))
        cnt = cnt + jnp.where(ahead, 1.0, 0.0)
    return jnp.where(cnt < float(TOP_N), 1.0, 0.0).astype(BF16)


def _gate_out(o_c, o_s, o_w, gates, g, nq, out_ref):
    for r in range(HPG):
        h = g * HPG + r
        sl = slice(r * nq, (r + 1) * nq)
        o = (o_c[sl] * gates[:, h:h + 1] + o_s[sl] * gates[:, N_HEADS + h:N_HEADS + h + 1]
             + o_w[sl] * gates[:, 2 * N_HEADS + h:2 * N_HEADS + h + 1])
        out_ref[0, :, h * HEAD_DIM:(h + 1) * HEAD_DIM] = o


def _inter_np(ncl, ns):
    n = np.arange(LANES)[:, None]
    s = np.arange(LANES)[None, :]
    c_start = n * CMP_STRIDE
    c_end = c_start + CMP_BLOCK - 1
    s_start = s * SEL_BLOCK
    m = (c_start < s_start + SEL_BLOCK) & (c_end >= s_start) & (n < ncl) & (s < ns)
    return m.astype(np.float32)


def _expand_np(nkeys):
    s = np.arange(LANES)[:, None]
    k = np.arange(nkeys)[None, :]
    return (k // SEL_BLOCK == s).astype(np.float32)


def _attn_prompt_kernel(q_ref, gt_ref, kc_ref, vc_ref, kvs_ref, wp_ref, tc_ref, ts_ref, tw_ref,
                        inter_ref, ex_ref, out_ref, *, ncmp, ns, nwk):
    i = pl.program_id(1)
    nq = Q_BLOCK
    rq = HPG * nq
    q = q_ref[0]
    gates = gt_ref[0]
    qrow = lax.broadcasted_iota(I32, (rq, 1), 0) % nq
    qpos = i * nq + qrow
    n_tiles = (i * nq + nq - 1) // SEL_KEY_TILE + 1
    qgs, o_cs, sels = [], [], []
    for g in range(N_KV):
        qg = _group_q(q, g)
        gl = slice(g * HEAD_DIM, (g + 1) * HEAD_DIM)
        lc = _dot_nt(qg, kc_ref[0, :, gl]) * SCALE + tc_ref[0, g]
        jc = lax.broadcasted_iota(I32, lc.shape, 1)
        valid_c = (qpos - (jc * CMP_STRIDE + CMP_BLOCK - 1) >= 0) & (jc < ncmp)
        pr_c = _masked_softmax(lc, valid_c)
        qgs.append(qg)
        o_cs.append(_dot(pr_c.astype(BF16), vc_ref[0, :, gl]))
        sels.append(_select_blocks(pr_c, inter_ref[...], i, nq, ns))

    def body(kt, carry):
        k0 = pl.multiple_of(kt * SEL_KEY_TILE, SEL_KEY_TILE)
        tb = jnp.minimum(i - kt * (SEL_KEY_TILE // nq), ts_ref.shape[0] - 1)
        kpos = k0 + lax.broadcasted_iota(I32, (rq, SEL_KEY_TILE), 1)
        causal = kpos <= qpos
        out = []
        for g in range(N_KV):
            m, l, acc = carry[g]
            k = kvs_ref[0, pl.ds(k0, SEL_KEY_TILE), g * HEAD_DIM:(g + 1) * HEAD_DIM]
            v = kvs_ref[0, pl.ds(k0, SEL_KEY_TILE), KV_DIM + g * HEAD_DIM:KV_DIM + (g + 1) * HEAD_DIM]
            lg = _dot_nt(qgs[g], k) * SCALE + ts_ref[tb, g]
            picked = _tile_rows(_dot(sels[g], ex_ref[:, pl.ds(k0, SEL_KEY_TILE)]))
            valid = (picked > 0.5) & causal
            lg = jnp.where(valid, lg, NEG)
            m_new = jnp.maximum(m, jnp.max(lg, axis=-1, keepdims=True))
            p = jnp.where(valid, jnp.exp(lg - m_new), 0.0)
            alpha = jnp.exp(m - m_new)
            l = alpha * l + jnp.sum(p, axis=-1, keepdims=True)
            acc = alpha * acc + _dot(p.astype(BF16), v)
            out.append((m_new, l, acc))
        return tuple(out)

    init = (jnp.full((rq, 1), NEG, F32), jnp.zeros((rq, 1), F32), jnp.zeros((rq, HEAD_DIM), F32))
    fin = lax.fori_loop(0, n_tiles, body, (init,) * N_KV)

    for g in range(N_KV):
        qg, o_c = qgs[g], o_cs[g]
        gl = slice(g * HEAD_DIM, (g + 1) * HEAD_DIM)
        o_s = fin[g][2] / fin[g][1]

        w0 = pl.multiple_of(i * nq, nq)
        kw = wp_ref[0, pl.ds(w0, nwk), gl]
        vw = wp_ref[0, pl.ds(w0, nwk), KV_DIM + g * HEAD_DIM:KV_DIM + (g + 1) * HEAD_DIM]
        lw = _dot_nt(qg, kw) * SCALE + tw_ref[g]
        jw = lax.broadcasted_iota(I32, lw.shape, 1)
        dist = WINDOW + qrow - jw
        valid_w = (i * nq - WINDOW + jw >= 0) & (dist >= 0) & (dist < WINDOW)
        o_w = _dot(_masked_softmax(lw, valid_w).astype(BF16), vw)
        _gate_out(o_c, o_s, o_w, gates, g, nq, out_ref)


def _attn_prompt(q, gates, kc, vc, kvb, rel_bias):
    b, l, _ = q.shape
    nq = Q_BLOCK
    nblk = l // nq
    ncl = kc.shape[1]
    ncmp = ncl - 1
    ns = max(-(-l // SEL_BLOCK), TOP_N)
    nwk = WINDOW + nq + (LANES - nq)
    qi = np.arange(nq)
    blk = np.arange(nblk)
    jc = np.arange(ncl)
    idx_c = _bucket_np(blk[:, None, None] * nq + qi[None, :, None] - (jc[None, None, :] * CMP_STRIDE + CMP_BLOCK - 1))
    n_ts = min(nblk, (MAX_DIST + SEL_KEY_TILE) // nq + 1)
    mt = np.arange(n_ts)
    kj = np.arange(SEL_KEY_TILE)
    idx_s = _bucket_np(mt[:, None, None] * nq + qi[None, :, None] - kj[None, None, :])
    jw = np.arange(nwk)
    idx_w = _bucket_np(WINDOW + qi[:, None] - jw[None, :])
    tc = _expand_bias(idx_c, rel_bias)
    ts = _expand_bias(idx_s, rel_bias)
    tw = _expand_bias(idx_w, rel_bias)
    inter = jnp.asarray(_inter_np(ncmp, ns)[:ncl], BF16)
    lk = -(-l // SEL_KEY_TILE) * SEL_KEY_TILE
    ex = jnp.asarray(_expand_np(lk), BF16)
    kvs = kvb[:, :, 0:2 * KV_DIM]
    if lk != l:
        kvs = jnp.pad(kvs, ((0, 0), (0, lk - l), (0, 0)))
    wp = jnp.pad(kvb[:, :, 2 * KV_DIM:4 * KV_DIM], ((0, 0), (WINDOW, nwk - WINDOW - nq), (0, 0)))
    lw = wp.shape[1]
    kern = functools.partial(_attn_prompt_kernel, ncmp=ncmp, ns=ns, nwk=nwk)
    return pl.pallas_call(
        kern, grid=(b, nblk), name="attn_prompt",
        in_specs=[pl.BlockSpec((1, nq, ATTN_DIM), lambda bi, i: (bi, i, 0)),
                  pl.BlockSpec((1, nq, LANES), lambda bi, i: (bi, i, 0)),
                  pl.BlockSpec((1, ncl, KV_DIM), lambda bi, i: (bi, 0, 0)),
                  pl.BlockSpec((1, ncl, KV_DIM), lambda bi, i: (bi, 0, 0)),
                  pl.BlockSpec((1, lk, 2 * KV_DIM), lambda bi, i: (bi, 0, 0)),
                  pl.BlockSpec((1, lw, 2 * KV_DIM), lambda bi, i: (bi, 0, 0)),
                  pl.BlockSpec((1, N_KV, HPG * nq, ncl), lambda bi, i: (i, 0, 0, 0)),
                  _const_spec(ts.shape), _const_spec(tw.shape), _const_spec(inter.shape), _const_spec(ex.shape)],
        out_specs=pl.BlockSpec((1, nq, ATTN_DIM), lambda bi, i: (bi, i, 0)),
        out_shape=jax.ShapeDtypeStruct((b, l, ATTN_DIM), F32),
        compiler_params=_cparams("arbitrary", "arbitrary"),
    )(q, gates, kc, vc, kvs, wp, tc, ts, tw, inter, ex)


def _attn_sample_kernel(pt_ref, *refs, n_pages, ns, nq, win_len):
    pages = refs[:n_pages]
    (rows_ref, wrows_ref, state_ref, q_ref, gt_ref, w1r_ref, pe_ref, w2_ref, kg_ref,
     tc_ref, ts_ref, tn_ref, tw_ref, inter_ref, ex_ref, perm_ref, out_ref, win_ref) = refs[n_pages:]
    del pt_ref
    page = pages[0].shape[1] // KV_ROWS
    cpp = page // CMP_STRIDE

    def page_rows(p, slot, g):
        return pages[p][0, pl.ds(slot * N_KV + g, page, stride=KV_ROWS), :].astype(BF16)
    past = n_pages * page
    ncl = past // CMP_STRIDE
    rq = HPG * nq
    q = q_ref[0]
    gates = gt_ref[0]
    qrow = lax.broadcasted_iota(I32, (rq, 1), 0) % nq
    cur = past // SEL_BLOCK
    pad_new = jnp.zeros((LANES - nq, HEAD_DIM), BF16)
    lane_new = lax.broadcasted_iota(I32, (rq, LANES), 1)
    valid_new = lane_new <= qrow

    win_ref[0, 0:(win_len - nq) * WIN_ROWS, :] = state_ref[0, nq * WIN_ROWS:win_len * WIN_ROWS, :]
    win_ref[0, (win_len - nq) * WIN_ROWS:win_len * WIN_ROWS, :] = wrows_ref[0]

    def new_rows(ref, nrows, slot, g):
        return jnp.concatenate([ref[0, pl.ds(slot * N_KV + g, nq, stride=nrows), :].astype(BF16), pad_new], axis=0)

    for g in range(N_KV):
        qg = _group_q(q, g)
        gl = slice(g * HEAD_DIM, (g + 1) * HEAD_DIM)

        def cmp_branch(br):
            by_s = [_dot(perm_ref[...], page_rows(p, br, g)).astype(BF16) for p in range(n_pages)]
            slabs = [jnp.concatenate([r[s * cpp:(s + 1) * cpp] for r in by_s], axis=0) for s in range(CMP_STRIDE)]
            return _compress_one(slabs, w1r_ref[br], pe_ref[br, 0:1, :], w2_ref[br])

        kc = _head_norm(cmp_branch(0), kg_ref[...]).astype(BF16)
        vc = cmp_branch(1).astype(BF16)
        lc = _dot_nt(qg, kc) * SCALE + tc_ref[g]
        jc = lax.broadcasted_iota(I32, lc.shape, 1)
        pr_c = _masked_softmax(lc, jc < ncl - 1)
        o_c = _dot(pr_c.astype(BF16), vc)
        sel = _select_blocks(pr_c, inter_ref[...], cur, nq, ns)

        picked = _tile_rows(_dot(sel, ex_ref[...]))
        lg = jnp.concatenate([_dot_nt(qg, page_rows(p, 2, g)) for p in range(n_pages)], axis=1) * SCALE + ts_ref[g]
        valid = picked > 0.5
        lg = jnp.where(valid, lg, NEG)
        k_new = new_rows(rows_ref, KV_ROWS, 2, g)
        v_new = new_rows(rows_ref, KV_ROWS, 3, g)
        ln = jnp.where(valid_new, _dot_nt(qg, k_new) * SCALE + tn_ref[g], NEG)
        m = jnp.maximum(jnp.max(lg, axis=-1, keepdims=True), jnp.max(ln, axis=-1, keepdims=True))
        p_old = jnp.where(valid, jnp.exp(lg - m), 0.0)
        p_new = jnp.where(valid_new, jnp.exp(ln - m), 0.0)
        den = jnp.sum(p_old, axis=-1, keepdims=True) + jnp.sum(p_new, axis=-1, keepdims=True)
        p_old = (p_old / den).astype(BF16)
        o_s = _dot((p_new / den).astype(BF16), v_new)
        for p in range(n_pages):
            o_s = o_s + _dot(p_old[:, p * page:(p + 1) * page], page_rows(p, 3, g))

        kw = state_ref[0, pl.ds(g, win_len, stride=WIN_ROWS), :].astype(BF16)
        vw = state_ref[0, pl.ds(N_KV + g, win_len, stride=WIN_ROWS), :].astype(BF16)
        lw = _dot_nt(qg, kw) * SCALE + tw_ref[g]
        jw = lax.broadcasted_iota(I32, lw.shape, 1)
        valid_w = jw > qrow + (win_len - WINDOW)
        lw = jnp.where(valid_w, lw, NEG)
        kwn = new_rows(wrows_ref, WIN_ROWS, 0, g)
        vwn = new_rows(wrows_ref, WIN_ROWS, 1, g)
        lwn = jnp.where(valid_new, _dot_nt(qg, kwn) * SCALE + tn_ref[g], NEG)
        m = jnp.maximum(jnp.max(lw, axis=-1, keepdims=True), jnp.max(lwn, axis=-1, keepdims=True))
        pw_old = jnp.where(valid_w, jnp.exp(lw - m), 0.0)
        pw_new = jnp.where(valid_new, jnp.exp(lwn - m), 0.0)
        den = jnp.sum(pw_old, axis=-1, keepdims=True) + jnp.sum(pw_new, axis=-1, keepdims=True)
        o_w = _dot((pw_old / den).astype(BF16), vw) + _dot((pw_new / den).astype(BF16), vwn)
        _gate_out(o_c, o_s, o_w, gates, g, nq, out_ref)


def _attn_sample(q, gates, rows, wrows, cache, state, page_table, w1r, pe_hid, w2, kc_gain, rel_bias):
    nb, nq, _ = q.shape
    n_pages = page_table.shape[1]
    page = cache.shape[1] // KV_ROWS
    past = n_pages * page
    win_len = state.shape[1] // WIN_ROWS
    t_all = past + nq
    ns = max(-(-t_all // SEL_BLOCK), TOP_N)
    ncl = past // CMP_STRIDE
    qi = np.arange(nq)
    idx_c = _bucket_np(past + qi[:, None] - (np.arange(ncl)[None, :] * CMP_STRIDE + CMP_BLOCK - 1))
    idx_s = _bucket_np(past + qi[:, None] - np.arange(past)[None, :])
    idx_n = _bucket_np(qi[:, None] - np.arange(LANES)[None, :])
    idx_w = _bucket_np(win_len + qi[:, None] - np.arange(win_len)[None, :])
    tc = _expand_bias(idx_c, rel_bias)
    ts = _expand_bias(idx_s, rel_bias)
    tn = _expand_bias(idx_n, rel_bias)
    tw = _expand_bias(idx_w, rel_bias)
    inter = jnp.asarray(_inter_np(ncl - 1, ns)[:ncl], BF16)
    ex = jnp.asarray(_expand_np(past), BF16)
    tok = np.arange(page)
    perm_np = ((tok % CMP_STRIDE) * (page // CMP_STRIDE) + tok // CMP_STRIDE)[None, :] == tok[:, None]
    perm = jnp.asarray(perm_np.astype(np.float32), BF16)
    kern = functools.partial(_attn_sample_kernel, n_pages=n_pages, ns=ns, nq=nq, win_len=win_len)
    page_specs = [pl.BlockSpec((1,) + cache.shape[1:], functools.partial(lambda bi, pt, k: (pt[bi, k], 0, 0), k=k))
                  for k in range(n_pages)]
    per_b = lambda shape: pl.BlockSpec((1,) + shape, lambda bi, pt: (bi, 0, 0))
    cst = lambda a: pl.BlockSpec(a.shape, lambda bi, pt: (0,) * a.ndim, pipeline_mode=pl.Buffered(1))
    grid_spec = pltpu.PrefetchScalarGridSpec(
        num_scalar_prefetch=1, grid=(nb,),
        in_specs=page_specs + [per_b((nq * KV_ROWS, LANES)), per_b((nq * WIN_ROWS, LANES)),
                               per_b((win_len * WIN_ROWS, LANES)),
                               per_b((nq, ATTN_DIM)), per_b((nq, LANES)),
                               cst(w1r), cst(pe_hid), cst(w2), cst(kc_gain),
                               cst(tc), cst(ts), cst(tn), cst(tw), cst(inter), cst(ex), cst(perm)],
        out_specs=[per_b((nq, ATTN_DIM)), per_b((win_len * WIN_ROWS, LANES))])
    return pl.pallas_call(
        kern, grid_spec=grid_spec, name="attn_sample",
        out_shape=[jax.ShapeDtypeStruct((nb, nq, ATTN_DIM), F32),
                   jax.ShapeDtypeStruct((nb, win_len * WIN_ROWS, LANES), F32)],
        compiler_params=_cparams("arbitrary"),
    )(page_table, *([cache] * n_pages), rows, wrows, state, q, gates, w1r, pe_hid, w2, kc_gain,
      tc, ts, tn, tw, inter, ex, perm)


def _merge_kernel(attn_ref, cn_ref, x_ref, ga_ref, w_ref, h_ref):
    a = _rms(attn_ref[...], ga_ref[...]).astype(BF16)
    ac = jnp.concatenate([a, cn_ref[...]], axis=1)
    h_ref[...] = x_ref[...] + _dot(ac, w_ref[...])


def _merge(attn, cn, x2, gain_a, w_out):
    n = x2.shape[0]
    tm = _token_tile(n)
    row = lambda w: pl.BlockSpec((tm, w), lambda i: (i, 0))
    return pl.pallas_call(
        _merge_kernel, grid=(n // tm,), name="merge",
        in_specs=[row(ATTN_DIM), row(CONV_DIM), row(D_MODEL), _const_spec((1, ATTN_DIM)), _const_spec(w_out.shape)],
        out_specs=row(D_MODEL),
        out_shape=jax.ShapeDtypeStruct((n, D_MODEL), F32),
        compiler_params=_cparams("arbitrary"),
    )(attn, cn, x2, gain_a, w_out)


def _top_rows(v, k, payload=None):
    nrow = v.shape[0]
    rid = lax.broadcasted_iota(I32, v.shape, 0).astype(F32)
    vals, ids = [], []
    for _ in range(k):
        m = jnp.max(v, axis=0, keepdims=True)
        am = jnp.min(jnp.where(v == m, rid, float(nrow)), axis=0, keepdims=True)
        hit = rid == am
        vals.append(m)
        if payload is None:
            ids.append(am)
        else:
            ids.append(jnp.max(jnp.where(hit, payload, -1.0), axis=0, keepdims=True))
        v = jnp.where(hit, -jnp.inf, v)
    return jnp.concatenate(vals, axis=0), jnp.concatenate(ids, axis=0)


def _route_kernel(h_ref, g2_ref, wq_ref, keys_ref, n2_ref, eidx_ref, gw_ref, qh_ref):
    n2 = _rms(h_ref[...], g2_ref[...])
    n2_ref[...] = n2
    qh_ref[...] = _dot(n2.astype(BF16), wq_ref[...])
    half = PEER_DK // 2

    def head(r, carry):
        sv, si = [], []
        for hf in range(2):
            c0 = pl.multiple_of(r * PEER_DK + hf * half, half)
            qc = qh_ref[:, pl.ds(c0, half)].astype(BF16)
            st = _dot_nt(keys_ref[hf], qc)
            v, ix = _top_rows(st, PEER_TOPK)
            sv.append(v)
            si.append(ix)
        sub = lax.broadcasted_iota(I32, (SUBLANES, st.shape[1]), 0)
        cand, cidx = [], []
        a = 0
        while PEER_TOPK // (a + 1) > 1:
            nb = PEER_TOPK // (a + 1)
            for b0 in range(0, nb, SUBLANES):
                s = sv[0][a:a + 1] + sv[1][b0:b0 + SUBLANES]
                cand.append(s if b0 + SUBLANES <= nb else jnp.where(sub < nb - b0, s, -jnp.inf))
                cidx.append(si[0][a:a + 1] * float(N_KEYS) + si[1][b0:b0 + SUBLANES])
            a += 1
        assert PEER_TOPK - a == SUBLANES
        cand.append(sv[0][a:PEER_TOPK] + sv[1][0:1])
        cidx.append(si[0][a:PEER_TOPK] * float(N_KEYS) + si[1][0:1])
        fv, fe = _top_rows(jnp.concatenate(cand, axis=0), PEER_TOPK, payload=jnp.concatenate(cidx, axis=0))
        e = jnp.exp(fv - fv[0:1])
        gw_ref[r] = e / jnp.sum(e, axis=0, keepdims=True)
        eidx_ref[r] = fe.astype(I32)
        return carry

    lax.fori_loop(0, PEER_HEADS, head, 0)


def _route(h, g2, wq, keys):
    n = h.shape[0]
    tm = _token_tile(n)
    row = lambda w: pl.BlockSpec((tm, w), lambda i: (i, 0))
    hk = pl.BlockSpec((PEER_HEADS, PEER_TOPK, tm), lambda i: (0, 0, i))
    return pl.pallas_call(
        _route_kernel, grid=(n // tm,), name="peer_route",
        in_specs=[row(D_MODEL), _const_spec((1, D_MODEL)), _const_spec(wq.shape), _const_spec(keys.shape)],
        out_specs=[row(D_MODEL), hk, hk],
        out_shape=[jax.ShapeDtypeStruct((n, D_MODEL), F32),
                   jax.ShapeDtypeStruct((PEER_HEADS, PEER_TOPK, n), I32),
                   jax.ShapeDtypeStruct((PEER_HEADS, PEER_TOPK, n), F32)],
        scratch_shapes=[pltpu.VMEM((tm, PEER_HEADS * PEER_DK), F32)],
        compiler_params=_cparams("arbitrary"),
    )(h, g2, wq, keys)


_BITREV = (0, 4, 2, 6, 1, 5, 3, 7)


def _sublane_sums(vs):
    sub = lax.broadcasted_iota(I32, (SUBLANES, LANES), 0)
    cur = [vs[j] for j in _BITREV]
    sh = SUBLANES // 2
    while sh >= 1:
        low = (sub & sh) == 0
        nxt = []
        for j in range(0, len(cur), 2):
            a, b = cur[j], cur[j + 1]
            nxt.append(jnp.where(low, a, pltpu.roll(b, sh, 0)) + jnp.where(low, pltpu.roll(a, SUBLANES - sh, 0), b))
        cur = nxt
        sh //= 2
    return cur[0]


def _peer_kernel(idx_cur_ref, idx_nxt_ref, x_ref, h_ref, gw_ref, tab_ref, out_ref, *scratch, nsteps):
    bufs, sem_ref = scratch[:PEER_PHASES], scratch[PEER_PHASES]
    i = pl.program_id(0)
    tt = PEER_TOKENS
    nrow = tt * PEER_PAIRS
    half = tab_ref.shape[1] // 2
    lane = lax.broadcasted_iota(I32, (SUBLANES, gw_ref.shape[1]), 1)
    tok0 = (i * PEER_PHASES * tt) % gw_ref.shape[1]

    def fetch(idx_row, j, dst_rows, sem, queue):
        pltpu.make_async_copy(tab_ref.at[idx_row[j]], dst_rows.at[j], sem).start(priority=queue)

    def wait_rows(dst, sem):
        pltpu.make_async_copy(tab_ref.at[pl.ds(0, nrow)], dst, sem).wait()

    def half_step(src, tok_off, idx_ref, idx_off, dst, dst_sem):
        def token(t, carry):
            x = x_ref[tok_off + t]
            x_lo = x[0:SUBLANES]
            x_hi = x[SUBLANES:2 * SUBLANES]
            my_lane = lane == tok0 + tok_off + t
            row0 = pl.multiple_of(t * PEER_PAIRS, PEER_PAIRS)
            src_rows = src.at[pl.ds(row0, PEER_PAIRS)]
            if dst is not None:
                dst_rows = dst.at[pl.ds(row0, PEER_PAIRS)]
                idx_row = idx_ref.at[0, 0, pl.ds(idx_off + row0, PEER_PAIRS)]
            acc_lo = jnp.zeros((SUBLANES, LANES), F32)
            acc_hi = jnp.zeros((SUBLANES, LANES), F32)
            for grp in range(PEER_PAIRS // SUBLANES):
                base = grp * SUBLANES
                prods = []
                for k in range(SUBLANES):
                    if dst is not None:
                        fetch(idx_row, base + k, dst_rows, dst_sem, k % 2)
                    u = src_rows[base + k, 0:half, :].astype(F32)
                    prods.append(u[0:SUBLANES] * x_lo + u[SUBLANES:2 * SUBLANES] * x_hi)
                act = jnp.sum(_sublane_sums(prods), axis=-1, keepdims=True)
                gsub = jnp.sum(jnp.where(my_lane, gw_ref[base:base + SUBLANES, :], 0.0), axis=-1, keepdims=True)
                coef = jnp.broadcast_to(_gelu(act) * gsub, (SUBLANES, LANES))
                for k in range(SUBLANES):
                    v = src_rows[base + k, half:2 * half, :].astype(F32)
                    ck = jnp.broadcast_to(coef[k:k + 1, :], (SUBLANES, LANES))
                    acc_lo = acc_lo + ck * v[0:SUBLANES]
                    acc_hi = acc_hi + ck * v[SUBLANES:2 * SUBLANES]
            out_ref[tok_off + t] = h_ref[tok_off + t] + jnp.concatenate([acc_lo, acc_hi], axis=0)
            return carry

        lax.fori_loop(0, tt, token, 0)

    ahead = PEER_PHASES // 2

    @pl.when(i == 0)
    def _():
        for k in range(ahead):
            def prime(j, carry, k=k):
                row = idx_cur_ref.at[0, 0, pl.ds(k * nrow, nrow)]
                fetch(row, 2 * j, bufs[k], sem_ref.at[k], 0)
                fetch(row, 2 * j + 1, bufs[k], sem_ref.at[k], 1)
                return carry
            lax.fori_loop(0, nrow // 2, prime, 0)

    for k in range(PEER_PHASES):
        wait_rows(bufs[k], sem_ref.at[k])
        nk = (k + ahead) % PEER_PHASES
        if k + ahead < PEER_PHASES:
            half_step(bufs[k], k * tt, idx_cur_ref, nk * nrow, bufs[nk], sem_ref.at[nk])
        else:
            @pl.when(i < nsteps - 1)
            def _(k=k, nk=nk):
                half_step(bufs[k], k * tt, idx_nxt_ref, nk * nrow, bufs[nk], sem_ref.at[nk])

            @pl.when(i == nsteps - 1)
            def _(k=k):
                half_step(bufs[k], k * tt, None, 0, None, None)


def _peer(n2, h, eidx, gw, tab):
    n = n2.shape[0]
    tstep = PEER_PHASES * PEER_TOKENS
    nsteps = n // tstep
    sub = D_MODEL // LANES
    x3 = n2.reshape(n, sub, LANES)
    h3 = h.reshape(n, sub, LANES)
    idx3 = eidx.reshape(nsteps, 1, tstep * PEER_PAIRS)
    tok = pl.BlockSpec((tstep, sub, LANES), lambda i: (i, 0, 0))
    smem = lambda f: pl.BlockSpec((1, 1, tstep * PEER_PAIRS), f, memory_space=pltpu.SMEM)
    gw_lanes = LANES if n % LANES == 0 else n
    rows = pltpu.VMEM((PEER_TOKENS * PEER_PAIRS,) + tab.shape[1:], tab.dtype)
    out = pl.pallas_call(
        functools.partial(_peer_kernel, nsteps=nsteps), grid=(nsteps,), name="peer_ffn",
        in_specs=[smem(lambda i: (i, 0, 0)), smem(lambda i: (jnp.minimum(i + 1, nsteps - 1), 0, 0)),
                  tok, tok,
                  pl.BlockSpec((PEER_PAIRS, gw_lanes), lambda i: (0, (i * tstep) // gw_lanes)),
                  pl.BlockSpec(memory_space=pl.ANY)],
        out_specs=tok,
        out_shape=jax.ShapeDtypeStruct((n, sub, LANES), F32),
        scratch_shapes=[rows] * PEER_PHASES + [pltpu.SemaphoreType.DMA((PEER_PHASES,))],
        compiler_params=_cparams("arbitrary"),
    )(idx3, idx3, x3, h3, gw, tab)
    return out.reshape(n, D_MODEL)


def _ple_kernel(h_ref, p_ref, g3_ref, wg_ref, wp_ref, y_ref):
    h = h_ref[...]
    gate = _sigmoid(_dot(_rms(h, g3_ref[...]).astype(BF16), wg_ref[...]))
    y_ref[...] = h + gate * _dot(p_ref[...].astype(BF16), wp_ref[...])


def _ple(h, p, g3, w_gate, w_ple):
    n = h.shape[0]
    tm = _token_tile(n)
    row = lambda w: pl.BlockSpec((tm, w), lambda i: (i, 0))
    return pl.pallas_call(
        _ple_kernel, grid=(n // tm,), name="ple",
        in_specs=[row(D_MODEL), row(PLE_DIM), _const_spec((1, D_MODEL)), _const_spec(w_gate.shape),
                  _const_spec(w_ple.shape)],
        out_specs=row(D_MODEL),
        out_shape=jax.ShapeDtypeStruct((n, D_MODEL), F32),
        compiler_params=_cparams("arbitrary"),
    )(h, p, g3, w_gate, w_ple)


def _tail(attn, cn, x2, p2, wts):
    h = _merge(attn, cn, x2, wts["gain_a"], wts["w_out"])
    n2, eidx_t, gw_t = _route(h, wts["g2"], wts["wq"], wts["keys"])
    n = h.shape[0]
    eidx = eidx_t.reshape(PEER_PAIRS, n).T
    gw = gw_t.reshape(PEER_PAIRS, n)
    h = _peer(n2, h, eidx, gw, wts["tab"])
    return _ple(h, p2, wts["g3"], wts["w_gate"], wts["w_ple"])


def kernel(x_prompt, x_sample, p_prompt, p_sample, cache_kv, state_win, state_conv, page_table, rel_bias, norm1,
           w_in, qk_gain, cmp_pe, cmp_w1, cmp_w2, conv_w, conv_b, out_gain, w_out, norm2, peer_wq, peer_keys,
           peer_u, peer_v, norm3, ple_gate, ple_proj):
    depth = norm1.shape[0]
    assert depth == 1
    l0 = 0
    b, l, d = x_prompt.shape
    nb, nq, _ = x_sample.shape
    n_pool, page = cache_kv.shape[1], cache_kv.shape[2]
    win_len = state_win.shape[2]

    w = w_in[l0]
    c_gt = ATTN_DIM + 6 * KV_DIM
    c_cv = c_gt + N_BRANCH * N_HEADS
    w_qkv = w[:, 0:c_gt].astype(BF16)
    w_cv = jnp.concatenate([w[:, c_cv:], jnp.pad(w[:, c_gt:c_cv], ((0, 0), (0, LANES - N_BRANCH * N_HEADS)))],
                           axis=1).astype(BF16)
    g1 = norm1[l0].reshape(1, d)
    w1 = cmp_w1[l0]
    half_k = CMP_STRIDE * HEAD_DIM
    w1r = jnp.concatenate([w1[:, 0:half_k], w1[:, half_k:2 * half_k]], axis=2).astype(BF16)
    w2 = cmp_w2[l0].astype(BF16)
    pe_hid = _pe_hid(cmp_pe[l0], w1)
    kc_gain = qk_gain[l0, 1].reshape(1, HEAD_DIM)
    wts = dict(
        gain_a=out_gain[l0, 0:ATTN_DIM].reshape(1, ATTN_DIM), w_out=w_out[l0].astype(BF16),
        g2=norm2[l0].reshape(1, d), wq=peer_wq[l0].astype(BF16), keys=peer_keys[l0].astype(BF16),
        tab=jnp.concatenate([peer_u[l0].astype(BF16).reshape(-1, d // LANES, LANES),
                             peer_v[l0].astype(BF16).reshape(-1, d // LANES, LANES)], axis=1),
        g3=norm3[l0].reshape(1, d), w_gate=ple_gate[l0].astype(BF16), w_ple=ple_proj[l0].astype(BF16))
    gain_c = out_gain[l0, ATTN_DIM:].reshape(1, CONV_DIM)
    cw = conv_w[l0]
    cb = conv_b[l0].reshape(1, CONV_DIM)

    xp = x_prompt.reshape(b * l, d)
    q, rows, wrows, kvb, bu, gates = _mix(xp, g1, w_qkv, w_cv, qk_gain[l0])
    kc, vc = _compress_prompt(rows.reshape(b, l * KV_ROWS, LANES), w1r, pe_hid, w2, kc_gain)
    attn = _attn_prompt(q.reshape(b, l, ATTN_DIM), gates.reshape(b, l, LANES), kc, vc,
                        kvb.reshape(b, l, 4 * KV_DIM), rel_bias)
    cn, cst_p = _conv(bu.reshape(b, l, 2 * CONV_DIM), jnp.zeros((b, CONV_WIDTH - 1, CONV_DIM), F32), cw, cb, gain_c)
    y_p = _tail(attn.reshape(b * l, ATTN_DIM), cn.reshape(b * l, CONV_DIM), xp,
                p_prompt[l0].reshape(b * l, PLE_DIM), wts)
    wl = min(WINDOW, l)
    kv_p = rows.reshape(1, b, l, 4, N_KV, HEAD_DIM)
    win_p = wrows.reshape(b, l, 2, N_KV, HEAD_DIM)[None, :, l - wl:]
    conv_p = cst_p[None]

    xs = x_sample.reshape(nb * nq, d)
    q, rows, wrows, kvb, bu, gates = _mix(xs, g1, w_qkv, w_cv, qk_gain[l0])
    attn, win_s = _attn_sample(q.reshape(nb, nq, ATTN_DIM), gates.reshape(nb, nq, LANES),
                               rows.reshape(nb, nq * KV_ROWS, LANES), wrows.reshape(nb, nq * WIN_ROWS, LANES),
                               cache_kv[l0].reshape(n_pool, page * KV_ROWS, LANES),
                               state_win[l0].reshape(nb, win_len * WIN_ROWS, LANES), page_table,
                               w1r, pe_hid, w2, kc_gain, rel_bias)
    cn, cst_s = _conv(bu.reshape(nb, nq, 2 * CONV_DIM), state_conv[l0], cw, cb, gain_c)
    y_s = _tail(attn.reshape(nb * nq, ATTN_DIM), cn.reshape(nb * nq, CONV_DIM), xs,
                p_sample[l0].reshape(nb * nq, PLE_DIM), wts)
    kv_s = rows.reshape(1, nb, nq, 4, N_KV, HEAD_DIM)
    win_s = win_s.reshape(1, nb, win_len, 2, N_KV, HEAD_DIM)
    conv_s = cst_s[None]

    return (y_p.reshape(b, l, d), y_s.reshape(nb, nq, d), kv_p, win_p, conv_p, kv_s, win_s, conv_s)
```
